```python
import jax, jax.numpy as jnp
from jax import lax
import numpy as np

D_MODEL = 4096
BATCH = 1
SEQ = 16384
DEPTH = 1
DEC_BATCH = 8
DEC_SEQ = 2048
PAST_LEN = 128

HEAD_DIM = 128
N_Q_HEADS = 16
N_KV_HEADS = 4
Q_PER_KV = N_Q_HEADS // N_KV_HEADS
ATTN_W = N_Q_HEADS * HEAD_DIM
KV_W = N_KV_HEADS * HEAD_DIM
WINDOW = 128
BLOCK = 128
D_RNN = D_MODEL
N_RNN_BLOCKS = 16
RNN_BLOCK = D_RNN // N_RNN_BLOCKS
RNN_CONV = 4
LRU_C = 8.0
D_FF = 3 * D_MODEL
FFN_CONV = 3
PLE_DIM = 256
IN_W = ATTN_W + 2 * KV_W + 2 * D_RNN
SPLITS = [ATTN_W, ATTN_W + KV_W, ATTN_W + 2 * KV_W, ATTN_W + 2 * KV_W + D_RNN]
EPS = 1e-6
NEG = -1e30

kernel_name = "hybrid_bidir_local_gqa_rglru_encoder"


def rms_norm(x, g):
    xf = x.astype(jnp.float32)
    y = xf * lax.rsqrt(jnp.mean(xf * xf, axis=-1, keepdims=True) + EPS)
    return (y * g.astype(jnp.float32)).astype(x.dtype)


def dw_conv(x, w, b):
    width = w.shape[0]
    left = (width - 1) // 2
    t = x.shape[1]
    xp = jnp.pad(x, ((0, 0), (left, width - 1 - left), (0, 0)))
    out = xp[:, 0:t] * w[0]
    for k in range(1, width):
        out = out + xp[:, k:k + t] * w[k]
    return out + b


def alibi_slopes():
    return jnp.asarray(2.0 ** (-8.0 * (np.arange(N_Q_HEADS) + 1) / N_Q_HEADS), jnp.float32)


def banded_attention(q, k, v, sink):
    b, t = q.shape[0], q.shape[1]
    nb = t // BLOCK
    qb = q.reshape(b, nb, BLOCK, N_KV_HEADS, Q_PER_KV, HEAD_DIM)

    def windows(z):
        zp = jnp.pad(z, ((0, 0), (BLOCK, BLOCK), (0, 0), (0, 0))).reshape(b, nb + 2, BLOCK, N_KV_HEADS, HEAD_DIM)
        return jnp.concatenate([zp[:, :-2], zp[:, 1:-1], zp[:, 2:]], axis=2)

    kw = windows(k)
    vw = windows(v)
    rel = jnp.arange(3 * BLOCK)[None, :] - BLOCK - jnp.arange(BLOCK)[:, None]
    key_pos = (jnp.arange(nb)[:, None] - 1) * BLOCK + jnp.arange(3 * BLOCK)[None, :]
    valid = (key_pos >= 0) & (key_pos < t)
    mask = (jnp.abs(rel) <= WINDOW)[None] & valid[:, None, :]
    dist = jnp.abs(rel).astype(jnp.float32)
    slopes = alibi_slopes().reshape(N_KV_HEADS, Q_PER_KV)[:, :, None, None]
    s = jnp.einsum('bnqhgd,bnkhd->bnhgqk', qb, kw).astype(jnp.float32) * (HEAD_DIM ** -0.5)
    s = s - slopes * dist
    s = jnp.where(mask[None, :, None, None], s, NEG)
    sk = sink.astype(jnp.float32).reshape(N_KV_HEADS, Q_PER_KV)[:, :, None, None]
    m = jnp.maximum(jnp.max(s, axis=-1, keepdims=True), sk)
    e = jnp.exp(s - m)
    pr = e / (jnp.sum(e, axis=-1, keepdims=True) + jnp.exp(sk - m))
    out = jnp.einsum('bnhgqk,bnkhd->bnqhgd', pr.astype(v.dtype), vw)
    return out.reshape(b, t, ATTN_W)


def linear_scan(a, bterm, reverse):
    def step(h, ab):
        h = ab[0] * h + ab[1]
        return h, h
    init = jnp.zeros((a.shape[0], a.shape[2]), jnp.float32)
    _, hs = lax.scan(step, init, (jnp.swapaxes(a, 0, 1), jnp.swapaxes(bterm, 0, 1)), reverse=reverse)
    return jnp.swapaxes(hs, 0, 1)


def rg_lru_bidir(xc, wa, ba, wx, bx, lam):
    b, t, _ = xc.shape
    xb = xc.reshape(b, t, N_RNN_BLOCKS, RNN_BLOCK)
    xf = xc.astype(jnp.float32)
    total = None
    for d in range(2):
        r = jax.nn.sigmoid(jnp.einsum('btnc,ncd->btnd', xb, wa[d]).reshape(b, t, D_RNN).astype(jnp.float32) + ba[d].astype(jnp.float32))
        ig = jax.nn.sigmoid(jnp.einsum('btnc,ncd->btnd', xb, wx[d]).reshape(b, t, D_RNN).astype(jnp.float32) + bx[d].astype(jnp.float32))
        log_a = -LRU_C * r * jax.nn.softplus(-lam[d].astype(jnp.float32))
        a = jnp.exp(log_a)
        bterm = jnp.sqrt(-jnp.expm1(2.0 * log_a)) * (ig * xf)
        h = linear_scan(a, bterm, reverse=(d == 1))
        total = h if total is None else total + h
    return total


def encoder(x, p, g_mix, w_in, rnn_conv_w, rnn_conv_b, lru_wa, lru_ba, lru_wx, lru_bx, lru_lambda,
            attn_sink, w_o_attn, w_o_rnn, w_merge_gate, w_out, g_ffn, w_up, ffn_conv_w, ffn_conv_b,
            w_down, w_ple_gate, w_ple, g_ple, g_final):
    b, t, _ = x.shape
    for l in range(DEPTH):
        h = rms_norm(x, g_mix[l])
        z = h @ w_in[l]
        q, k, v, xr, yr = jnp.split(z, SPLITS, axis=-1)
        attn = banded_attention(q.reshape(b, t, N_Q_HEADS, HEAD_DIM),
                                k.reshape(b, t, N_KV_HEADS, HEAD_DIM),
                                v.reshape(b, t, N_KV_HEADS, HEAD_DIM), attn_sink[l])
        xc = dw_conv(xr, rnn_conv_w[l], rnn_conv_b[l])
        rec = (rg_lru_bidir(xc, lru_wa[l], lru_ba[l], lru_wx[l], lru_bx[l], lru_lambda[l])
               * jax.nn.gelu(yr.astype(jnp.float32))).astype(x.dtype)
        gates = jax.nn.sigmoid((h @ w_merge_gate[l]).astype(jnp.float32))
        merged = (gates[..., :D_MODEL] * (attn @ w_o_attn[l]).astype(jnp.float32)
                  + gates[..., D_MODEL:] * (rec @ w_o_rnn[l]).astype(jnp.float32))
        x = x + merged.astype(x.dtype) @ w_out[l]
        u = dw_conv(rms_norm(x, g_ffn[l]) @ w_up[l], ffn_conv_w[l], ffn_conv_b[l])
        x = x + (jax.nn.gelu(u[..., D_FF:]) * u[..., :D_FF]) @ w_down[l]
        e = rms_norm(p[l].astype(x.dtype) @ w_ple[l], g_ple[l])
        x = x + jax.nn.sigmoid(x @ w_ple_gate[l]) * e
    return rms_norm(x, g_final)


def setup_inputs(seed: int = 0) -> dict:
    key = jax.random.key(seed)
    ks = jax.random.split(key, 32)

    def nrm(k, shape, fan_in):
        return jax.random.normal(k, shape, jnp.float32) * (fan_in ** -0.5)

    def gain(k, shape):
        return 1.0 + 0.05 * jax.random.normal(k, shape, jnp.float32)

    u = jax.random.uniform(ks[10], (DEPTH, 2, D_RNN), jnp.float32, minval=0.9, maxval=0.999)
    a_base = u ** (1.0 / LRU_C)
    lam = jnp.log(a_base) - jnp.log1p(-a_base)
    return {
        "x_prompt": jax.random.normal(ks[0], (BATCH, SEQ, D_MODEL), jnp.float32),
        "x_sample": jax.random.normal(ks[1], (DEC_BATCH, DEC_SEQ, D_MODEL), jnp.float32),
        "p_prompt": jax.random.normal(ks[2], (DEPTH, BATCH, SEQ, PLE_DIM), jnp.float32),
        "p_sample": jax.random.normal(ks[3], (DEPTH, DEC_BATCH, DEC_SEQ, PLE_DIM), jnp.float32),
        "g_mix": gain(ks[4], (DEPTH, D_MODEL)),
        "w_in": nrm(ks[5], (DEPTH, D_MODEL, IN_W), D_MODEL),
        "rnn_conv_w": nrm(ks[6], (DEPTH, RNN_CONV, D_RNN), RNN_CONV),
        "rnn_conv_b": 0.01 * jax.random.normal(ks[7], (DEPTH, D_RNN), jnp.float32),
        "lru_wa": nrm(ks[8], (DEPTH, 2, N_RNN_BLOCKS, RNN_BLOCK, RNN_BLOCK), RNN_BLOCK),
        "lru_ba": 0.01 * jax.random.normal(ks[9], (DEPTH, 2, D_RNN), jnp.float32),
        "lru_wx": nrm(ks[11], (DEPTH, 2, N_RNN_BLOCKS, RNN_BLOCK, RNN_BLOCK), RNN_BLOCK),
        "lru_bx": 0.01 * jax.random.normal(ks[12], (DEPTH, 2, D_RNN), jnp.float32),
        "lru_lambda": lam,
        "attn_sink": jax.random.normal(ks[13], (DEPTH, N_Q_HEADS), jnp.float32),
        "w_o_attn": nrm(ks[14], (DEPTH, ATTN_W, D_MODEL), ATTN_W),
        "w_o_rnn": nrm(ks[15], (DEPTH, D_RNN, D_MODEL), D_RNN),
        "w_merge_gate": nrm(ks[16], (DEPTH, D_MODEL, 2 * D_MODEL), D_MODEL),
        "w_out": nrm(ks[17], (DEPTH, D_MODEL, D_MODEL), D_MODEL),
        "g_ffn": gain(ks[18], (DEPTH, D_MODEL)),
        "w_up": nrm(ks[19], (DEPTH, D_MODEL, 2 * D_FF), D_MODEL),
        "ffn_conv_w": nrm(ks[20], (DEPTH, FFN_CONV, 2 * D_FF), FFN_CONV),
        "ffn_conv_b": 0.01 * jax.random.normal(ks[21], (DEPTH, 2 * D_FF), jnp.float32),
        "w_down": nrm(ks[22], (DEPTH, D_FF, D_MODEL), D_FF),
        "w_ple_gate": nrm(ks[23], (DEPTH, D_MODEL, D_MODEL), D_MODEL),
        "w_ple": nrm(ks[24], (DEPTH, PLE_DIM, D_MODEL), PLE_DIM),
        "g_ple": gain(ks[25], (DEPTH, D_MODEL)),
        "g_final": gain(ks[26], (D_MODEL,)),
    }


def reference(x_prompt, x_sample, p_prompt, p_sample, g_mix, w_in, rnn_conv_w, rnn_conv_b, lru_wa, lru_ba,
              lru_wx, lru_bx, lru_lambda, attn_sink, w_o_attn, w_o_rnn, w_merge_gate, w_out, g_ffn, w_up,
              ffn_conv_w, ffn_conv_b, w_down, w_ple_gate, w_ple, g_ple, g_final):
    y_prompt = encoder(x_prompt, p_prompt, g_mix, w_in, rnn_conv_w, rnn_conv_b, lru_wa, lru_ba, lru_wx, lru_bx,
                       lru_lambda, attn_sink, w_o_attn, w_o_rnn, w_merge_gate, w_out, g_ffn, w_up, ffn_conv_w,
                       ffn_conv_b, w_down, w_ple_gate, w_ple, g_ple, g_final)
    y_sample = encoder(x_sample, p_sample, g_mix, w_in, rnn_conv_w, rnn_conv_b, lru_wa, lru_ba, lru_wx, lru_bx,
                       lru_lambda, attn_sink, w_o_attn, w_o_rnn, w_merge_gate, w_out, g_ffn, w_up, ffn_conv_w,
                       ffn_conv_b, w_down, w_ple_gate, w_ple, g_ple, g_final)
    return (y_prompt, y_sample)
```

```python
import functools

import numpy as np
import jax
import jax.numpy as jnp
from jax import lax
from jax.experimental import pallas as pl
from jax.experimental.pallas import tpu as pltpu

F32 = jnp.float32
BF16 = jnp.bfloat16

EPS = 1e-6
NEG = -1e30
ATTN_BLOCK = 128
LRU_C = 8.0
RNN_HALO = 8
FFN_HALO = 16
V7X_VMEM_LIMIT_BYTES = 58 * 1024 * 1024


def _cparams(*sem):
    return pltpu.CompilerParams(dimension_semantics=sem, vmem_limit_bytes=V7X_VMEM_LIMIT_BYTES)


def _tile(dim, pref):
    t = min(dim, pref)
    while dim % t:
        t //= 2
    return t


def _rmsnorm_kernel(x_ref, g_ref, o_ref):
    x = x_ref[...].astype(F32)
    ms = jnp.mean(x * x, axis=-1, keepdims=True)
    o_ref[...] = (x * lax.rsqrt(ms + EPS) * g_ref[...]).astype(o_ref.dtype)


def _rmsnorm(x, g, out_dtype, tm=256):
    m, d = x.shape
    tm = _tile(m, tm)
    return pl.pallas_call(
        _rmsnorm_kernel,
        grid=(m // tm,),
        in_specs=[pl.BlockSpec((tm, d), lambda i: (i, 0)), pl.BlockSpec((1, d), lambda i: (0, 0))],
        out_specs=pl.BlockSpec((tm, d), lambda i: (i, 0)),
        out_shape=jax.ShapeDtypeStruct((m, d), out_dtype),
        compiler_params=_cparams("parallel"),
    )(x, g.reshape(1, d).astype(F32))


def _mm_kernel(*refs, n_a, n_tile, n_row, epilogue):
    a_refs = refs[:n_a]
    b_refs = refs[n_a:2 * n_a]
    t_refs = refs[2 * n_a:2 * n_a + n_tile]
    r_refs = refs[2 * n_a + n_tile:2 * n_a + n_tile + n_row]
    o_ref = refs[-1]
    accs = [jnp.dot(a[...], b[...], preferred_element_type=F32) for a, b in zip(a_refs, b_refs)]
    o_ref[...] = epilogue(accs, [t[...] for t in t_refs], [r[...] for r in r_refs]).astype(o_ref.dtype)


def _mm(a_list, b_list, epilogue, out_dtype, tile_extras=(), row_extras=(), tm=1024, tn=1024):
    m = a_list[0].shape[0]
    n = b_list[0].shape[1]
    tm = _tile(m, tm)
    tn = _tile(n, tn)
    in_specs = (
        [pl.BlockSpec((tm, a.shape[1]), lambda i, j: (i, 0)) for a in a_list]
        + [pl.BlockSpec((b.shape[0], tn), lambda i, j: (0, j)) for b in b_list]
        + [pl.BlockSpec((tm, tn), lambda i, j: (i, j)) for _ in tile_extras]
        + [pl.BlockSpec((1, tn), lambda i, j: (0, j)) for _ in row_extras]
    )
    kern = functools.partial(_mm_kernel, n_a=len(a_list), n_tile=len(tile_extras), n_row=len(row_extras),
                             epilogue=epilogue)
    return pl.pallas_call(
        kern,
        grid=(m // tm, n // tn),
        in_specs=in_specs,
        out_specs=pl.BlockSpec((tm, tn), lambda i, j: (i, j)),
        out_shape=jax.ShapeDtypeStruct((m, n), out_dtype),
        compiler_params=_cparams("parallel", "arbitrary"),
    )(*a_list, *b_list, *tile_extras, *row_extras)


def _attn_kernel(sink_ref, q_ref, kp_ref, kc_ref, kn_ref, vp_ref, vc_ref, vn_ref, o_ref, bias_scr,
                 *, nb, n_kv, group, hd, slopes):
    blk = ATTN_BLOCK
    n = pl.program_id(1)

    @pl.when((pl.program_id(0) == 0) & (n == 0))
    def _():
        t = lax.broadcasted_iota(jnp.int32, (blk, 3 * blk), 0)
        s = lax.broadcasted_iota(jnp.int32, (blk, 3 * blk), 1)
        dist = jnp.abs(s - blk - t)
        distf = dist.astype(F32)
        for hq in range(n_kv * group):
            bias_scr[hq // group, (hq % group) * blk:(hq % group + 1) * blk, :] = jnp.where(
                dist <= blk, -slopes[hq] * distf, NEG)

    col = lax.broadcasted_iota(jnp.int32, (1, 3 * blk), 1)
    edge = jnp.where(((col < blk) & (n == 0)) | ((col >= 2 * blk) & (n == nb - 1)), NEG, 0.0).astype(F32)
    scale = hd ** -0.5

    for h in range(n_kv):
        q_h = jnp.concatenate(
            [q_ref[:, (h * group + g) * hd:(h * group + g + 1) * hd] for g in range(group)], axis=0)
        k_h = jnp.concatenate([r[:, h * hd:(h + 1) * hd] for r in (kp_ref, kc_ref, kn_ref)], axis=0)
        v_h = jnp.concatenate([r[:, h * hd:(h + 1) * hd] for r in (vp_ref, vc_ref, vn_ref)], axis=0)
        s = lax.dot_general(q_h, k_h, (((1,), (1,)), ((), ())), preferred_element_type=F32)
        s = s * scale + bias_scr[h] + edge
        sk = jnp.concatenate(
            [jnp.full((blk, 1), sink_ref[h * group + g], F32) for g in range(group)], axis=0)
        m = jnp.maximum(jnp.max(s, axis=-1, keepdims=True), sk)
        e = jnp.exp(s - m)
        denom = jnp.sum(e, axis=-1, keepdims=True) + jnp.exp(sk - m)
        o = jnp.dot(e.astype(BF16), v_h, preferred_element_type=F32) * (1.0 / denom)
        for g in range(group):
            o_ref[:, (h * group + g) * hd:(h * group + g + 1) * hd] = o[g * blk:(g + 1) * blk].astype(o_ref.dtype)


def _attention(qkv, sink, batch, seq, n_q, n_kv, hd):
    blk = ATTN_BLOCK
    nb = seq // blk
    attn_w, kv_w = n_q * hd, n_kv * hd
    assert attn_w % kv_w == 0 and seq % blk == 0
    kcol = attn_w // kv_w
    slopes = tuple(float(v) for v in (2.0 ** (-8.0 * (np.arange(n_q) + 1) / n_q)).astype(np.float32))

    def rows(shift):
        return lambda b, n: (b * nb + jnp.clip(n + shift, 0, nb - 1))

    def spec(w, colblk, shift):
        r = rows(shift)
        return pl.BlockSpec((blk, w), lambda b, n: (r(b, n), colblk))

    kern = functools.partial(_attn_kernel, nb=nb, n_kv=n_kv, group=n_q // n_kv, hd=hd, slopes=slopes)
    return pl.pallas_call(
        kern,
        grid=(batch, nb),
        in_specs=[pl.BlockSpec(memory_space=pltpu.SMEM),
                  spec(attn_w, 0, 0),
                  spec(kv_w, kcol, -1), spec(kv_w, kcol, 0), spec(kv_w, kcol, 1),
                  spec(kv_w, kcol + 1, -1), spec(kv_w, kcol + 1, 0), spec(kv_w, kcol + 1, 1)],
        out_specs=pl.BlockSpec((blk, attn_w), lambda b, n: (b * nb + n, 0)),
        out_shape=jax.ShapeDtypeStruct((batch * seq, attn_w), BF16),
        scratch_shapes=[pltpu.VMEM((n_kv, (n_q // n_kv) * blk, 3 * blk), F32)],
        compiler_params=_cparams("arbitrary", "arbitrary"),
    )(sink.astype(F32), qkv, qkv, qkv, qkv, qkv, qkv, qkv)


def _rnn_kernel(*refs, reverse, n_tt, tt_rows, n_sub, sub_w):
    if reverse:
        (xm_ref, xp_ref, xn_ref, cw_ref, cb_ref, wa_ref, wx_ref, ba_ref, bx_ref, lam_ref,
         hf_ref, gy_ref, o_ref, a_scr, b_scr, carry_scr) = refs
    else:
        (xm_ref, xp_ref, xn_ref, cw_ref, cb_ref, wa_ref, wx_ref, ba_ref, bx_ref, lam_ref,
         o_ref, a_scr, b_scr, carry_scr) = refs
    halo = RNN_HALO
    tt = pl.program_id(2)
    tte = (n_tt - 1 - tt) if reverse else tt

    @pl.when(tt == 0)
    def _():
        carry_scr[...] = jnp.zeros_like(carry_scr)

    xm = xm_ref[...]
    xp = jnp.where(tte > 0, xp_ref[...], 0.0)
    xn = jnp.where(tte < n_tt - 1, xn_ref[...], 0.0)
    ext = jnp.concatenate([xp, xm, xn], axis=0)
    cw = cw_ref[...]
    xc = ext[halo - 1:halo - 1 + tt_rows] * cw[0:1]
    for k in range(1, 4):
        xc = xc + ext[halo - 1 + k:halo - 1 + k + tt_rows] * cw[k:k + 1]
    xc = xc + cb_ref[...]

    xcb = xc.astype(BF16)
    r_pre = jnp.concatenate(
        [jnp.dot(xcb[:, s * sub_w:(s + 1) * sub_w], wa_ref[s], preferred_element_type=F32) for s in range(n_sub)],
        axis=1)
    i_pre = jnp.concatenate(
        [jnp.dot(xcb[:, s * sub_w:(s + 1) * sub_w], wx_ref[s], preferred_element_type=F32) for s in range(n_sub)],
        axis=1)
    r = jax.nn.sigmoid(r_pre + ba_ref[...])
    ig = jax.nn.sigmoid(i_pre + bx_ref[...])
    log_a = (-LRU_C) * r * jax.nn.softplus(-lam_ref[...])
    a = jnp.exp(log_a)
    one_minus_a2 = jnp.tanh(-log_a) * (1.0 + a * a)
    a_scr[...] = a.reshape(a_scr.shape)
    b_scr[...] = (jnp.sqrt(one_minus_a2) * (ig * xc)).reshape(b_scr.shape)

    n_slab = tt_rows // 8
    row = lax.broadcasted_iota(jnp.int32, (8, a_scr.shape[-1]), 0)

    def slab(i, carry):
        idx = (n_slab - 1 - i) if reverse else i
        a = a_scr[idx]
        b = b_scr[idx]
        for k in (1, 2, 4):
            if reverse:
                keep = row < 8 - k
                a_sh = jnp.where(keep, pltpu.roll(a, 8 - k, 0), 1.0)
                b_sh = jnp.where(keep, pltpu.roll(b, 8 - k, 0), 0.0)
            else:
                keep = row >= k
                a_sh = jnp.where(keep, pltpu.roll(a, k, 0), 1.0)
                b_sh = jnp.where(keep, pltpu.roll(b, k, 0), 0.0)
            b = b + a * b_sh
            a = a * a_sh
        h = a * carry + b
        b_scr[idx] = h
        last = h[0:1] if reverse else h[7:8]
        return jnp.broadcast_to(last, h.shape)

    carry_scr[...] = lax.fori_loop(0, n_slab, slab, carry_scr[...], unroll=2)
    hs = b_scr[...].reshape(tt_rows, -1)
    if reverse:
        o_ref[...] = ((hf_ref[...] + hs) * gy_ref[...]).astype(o_ref.dtype)
    else:
        o_ref[...] = hs


def _rnn_direction(d, xr, conv_w, conv_b, wa, wx, ba, bx, lam, batch, seq, hf=None, gy=None, tt_rows=512, cb=512):
    m, d_rnn = xr.shape
    n_blk, sub_w = wa.shape[1], wa.shape[2]
    reverse = d == 1
    halo = RNN_HALO
    tt_rows = _tile(seq, tt_rows)
    cb = max(_tile(d_rnn, cb), sub_w)
    n_sub = cb // sub_w
    n_tt = seq // tt_rows
    hb = tt_rows // halo
    n_hb = seq // halo

    def te(t):
        return (n_tt - 1 - t) if reverse else t

    main = pl.BlockSpec((tt_rows, cb), lambda b, c, t: (b * n_tt + te(t), c))
    prev = pl.BlockSpec((halo, cb), lambda b, c, t: (b * n_hb + jnp.maximum(te(t) * hb - 1, 0), c))
    nxt = pl.BlockSpec((halo, cb), lambda b, c, t: (b * n_hb + jnp.minimum((te(t) + 1) * hb, n_hb - 1), c))
    vec = lambda rows: pl.BlockSpec((rows, cb), lambda b, c, t: (0, c))
    wspec = pl.BlockSpec((n_sub, sub_w, sub_w), lambda b, c, t: (c, 0, 0))
    in_specs = [main, prev, nxt, vec(4), vec(1), wspec, wspec, vec(1), vec(1), vec(1)]
    args = [xr, xr, xr, conv_w.astype(F32), conv_b.reshape(1, -1).astype(F32), wa[d].astype(BF16),
            wx[d].astype(BF16), ba[d].reshape(1, -1).astype(F32), bx[d].reshape(1, -1).astype(F32),
            lam[d].reshape(1, -1).astype(F32)]
    if reverse:
        in_specs += [main, main]
        args += [hf, gy]
    kern = functools.partial(_rnn_kernel, reverse=reverse, n_tt=n_tt, tt_rows=tt_rows, n_sub=n_sub, sub_w=sub_w)
    return pl.pallas_call(
        kern,
        grid=(batch, d_rnn // cb, n_tt),
        in_specs=in_specs,
        out_specs=main,
        out_shape=jax.ShapeDtypeStruct((m, d_rnn), BF16 if reverse else F32),
        scratch_shapes=[pltpu.VMEM((tt_rows // 8, 8, cb), F32), pltpu.VMEM((tt_rows // 8, 8, cb), F32),
                        pltpu.VMEM((8, cb), F32)],
        compiler_params=_cparams("arbitrary", "arbitrary", "arbitrary"),
    )(*args)


def _ffn_kernel(hm_ref, hp_ref, hn_ref, wv_ref, wg_ref, cwv_ref, cwg_ref, cbv_ref, cbg_ref, wd_ref, o_ref,
                ext_scr, uv_scr, ug_scr, *, n_tt, tm):
    halo = FFN_HALO
    tt = pl.program_id(1)
    j = pl.program_id(2)

    @pl.when(j == 0)
    def _():
        ext_scr[0:halo, :] = jnp.where(tt > 0, hp_ref[...], jnp.zeros_like(hp_ref))
        ext_scr[halo:halo + tm, :] = hm_ref[...]
        ext_scr[halo + tm:, :] = jnp.where(tt < n_tt - 1, hn_ref[...], jnp.zeros_like(hn_ref))

    ext = ext_scr[...]
    uv_scr[...] = jnp.dot(ext, wv_ref[...], preferred_element_type=F32)
    ug_scr[...] = jnp.dot(ext, wg_ref[...], preferred_element_type=F32)

    def conv(u_scr, cw_ref, cb_ref):
        cw = cw_ref[...]
        out = u_scr[pl.ds(halo - 1, tm), :] * cw[0:1]
        out = out + u_scr[pl.ds(halo, tm), :] * cw[1:2]
        out = out + u_scr[pl.ds(halo + 1, tm), :] * cw[2:3]
        return out + cb_ref[...]

    val = conv(uv_scr, cwv_ref, cbv_ref)
    gate = conv(ug_scr, cwg_ref, cbg_ref)
    act = (jax.nn.gelu(gate) * val).astype(BF16)
    contrib = jnp.dot(act, wd_ref[...], preferred_element_type=F32)

    @pl.when(j == 0)
    def _():
        o_ref[...] = contrib

    @pl.when(j > 0)
    def _():
        o_ref[...] += contrib


def _ffn(hn, w_up, conv_w, conv_b, w_down, batch, seq, tm=512, fc=512):
    m, d = hn.shape
    d_ff = w_down.shape[0]
    halo = FFN_HALO
    tm = _tile(seq, tm)
    fc = _tile(d_ff, fc)
    n_tt = seq // tm
    n_fc = d_ff // fc
    hb = tm // halo
    n_hb = seq // halo
    main = pl.BlockSpec((tm, d), lambda b, t, j: (b * n_tt + t, 0), pipeline_mode=pl.Buffered(1))
    prev = pl.BlockSpec((halo, d), lambda b, t, j: (b * n_hb + jnp.maximum(t * hb - 1, 0), 0))
    nxt = pl.BlockSpec((halo, d), lambda b, t, j: (b * n_hb + jnp.minimum((t + 1) * hb, n_hb - 1), 0))
    wv =pl.BlockSpec((d, fc), lambda b, t, j: (0, j))
    wg = pl.BlockSpec((d, fc), lambda b, t, j: (0, n_fc + j))
    cv = lambda rows: pl.BlockSpec((rows, fc), lambda b, t, j: (0, j))
    cg = lambda rows: pl.BlockSpec((rows, fc), lambda b, t, j: (0, n_fc + j))
    wd = pl.BlockSpec((fc, d), lambda b, t, j: (j, 0))
    cb2 = conv_b.reshape(1, -1).astype(F32)
    cw = conv_w.astype(F32)
    kern = functools.partial(_ffn_kernel, n_tt=n_tt, tm=tm)
    return pl.pallas_call(
        kern,
        grid=(batch, n_tt, n_fc),
        in_specs=[main, prev, nxt, wv, wg, cv(3), cg(3), cv(1), cg(1), wd],
        out_specs=main,
        out_shape=jax.ShapeDtypeStruct((m, d), F32),
        scratch_shapes=[pltpu.VMEM((tm + 2 * halo, d), BF16), pltpu.VMEM((tm + 2 * halo, fc), F32),
                        pltpu.VMEM((tm + 2 * halo, fc), F32)],
        compiler_params=_cparams("parallel", "arbitrary", "arbitrary"),
    )(hn, hn, hn, w_up, w_up, cw, cw, cb2, cb2, w_down)


def _ple_e_kernel(p_ref, w_ref, g_ref, o_ref):
    pe = jnp.dot(p_ref[...].astype(BF16), w_ref[...], preferred_element_type=F32)
    ms = jnp.mean(pe * pe, axis=-1, keepdims=True)
    o_ref[...] = pe * lax.rsqrt(ms + EPS) * g_ref[...]


def _ple_e(p, w, g, tm=256):
    m, k = p.shape
    d = w.shape[1]
    tm = _tile(m, tm)
    return pl.pallas_call(
        _ple_e_kernel,
        grid=(m // tm,),
        in_specs=[pl.BlockSpec((tm, k), lambda i: (i, 0)), pl.BlockSpec((k, d), lambda i: (0, 0)),
                  pl.BlockSpec((1, d), lambda i: (0, 0))],
        out_specs=pl.BlockSpec((tm, d), lambda i: (i, 0)),
        out_shape=jax.ShapeDtypeStruct((m, d), F32),
        compiler_params=_cparams("parallel"),
    )(p, w, g.reshape(1, d).astype(F32))


def _add_cast_kernel(a_ref, b_ref, o_ref, ob_ref):
    s = a_ref[...] + b_ref[...]
    o_ref[...] = s
    ob_ref[...] = s.astype(BF16)


def _add_cast(a, b, tm=256):
    m, d = a.shape
    tm = _tile(m, tm)
    spec = pl.BlockSpec((tm, d), lambda i: (i, 0))
    return pl.pallas_call(
        _add_cast_kernel,
        grid=(m // tm,),
        in_specs=[spec, spec],
        out_specs=[spec, spec],
        out_shape=[jax.ShapeDtypeStruct((m, d), F32), jax.ShapeDtypeStruct((m, d), BF16)],
        compiler_params=_cparams("parallel"),
    )(a, b)


def _encoder(x3, p4, wts):
    batch, seq, d = x3.shape
    m = batch * seq
    x = x3.reshape(m, d)
    n_q = wts["attn_sink"].shape[-1]
    attn_w = wts["w_o_attn"].shape[-2]
    d_rnn = wts["w_o_rnn"].shape[-2]
    hd = attn_w // n_q
    kv_w = (wts["w_in"].shape[-1] - attn_w - 2 * d_rnn) // 2
    n_kv = kv_w // hd
    qkv_w = attn_w + 2 * kv_w
    depth = wts["w_in"].shape[0]

    for l in range(depth):
        w_in = wts["w_in"][l]
        h = _rmsnorm(x, wts["g_mix"][l], BF16)
        qkv = _mm([h], [w_in[:, :qkv_w].astype(BF16)], lambda acc, t, r: acc[0], BF16)
        xr = _mm([h], [w_in[:, qkv_w:qkv_w + d_rnn].astype(BF16)], lambda acc, t, r: acc[0], F32)
        gy = _mm([h], [w_in[:, qkv_w + d_rnn:].astype(BF16)], lambda acc, t, r: jax.nn.gelu(acc[0]), F32)
        gates = _mm([h], [wts["w_merge_gate"][l].astype(BF16)], lambda acc, t, r: jax.nn.sigmoid(acc[0]), F32)

        attn = _attention(qkv, wts["attn_sink"][l], batch, seq, n_q, n_kv, hd)

        rnn_args = (xr, wts["rnn_conv_w"][l], wts["rnn_conv_b"][l], wts["lru_wa"][l], wts["lru_wx"][l],
                    wts["lru_ba"][l], wts["lru_bx"][l], wts["lru_lambda"][l], batch, seq)
        h_fwd = _rnn_direction(0, *rnn_args)
        rec = _rnn_direction(1, *rnn_args, hf=h_fwd, gy=gy)

        merged = _mm_merge(attn, rec, wts["w_o_attn"][l].astype(BF16), wts["w_o_rnn"][l].astype(BF16), gates, d)
        x1 = _mm([merged], [wts["w_out"][l].astype(BF16)], lambda acc, t, r: t[0] + acc[0], F32, tile_extras=[x],
                 tn=512)

        hn = _rmsnorm(x1, wts["g_ffn"][l], BF16)
        f = _ffn(hn, wts["w_up"][l].astype(BF16), wts["ffn_conv_w"][l], wts["ffn_conv_b"][l],
                 wts["w_down"][l].astype(BF16), batch, seq)

        e = _ple_e(p4[l].reshape(m, -1), wts["w_ple"][l].astype(BF16), wts["g_ple"][l])
        x2, x2b = _add_cast(x1, f)
        x = _mm([x2b], [wts["w_ple_gate"][l].astype(BF16)],
                lambda acc, t, r: t[0] + jax.nn.sigmoid(acc[0]) * t[1], F32, tile_extras=[x2, e], tn=512)
    y = _rmsnorm(x, wts["g_final"], F32)
    return y.reshape(batch, seq, d)


def _merge_kernel(a_ref, r_ref, wa_ref, wr_ref, ga_ref, gr_ref, o_ref):
    pa = jnp.dot(a_ref[...], wa_ref[...], preferred_element_type=F32)
    pr = jnp.dot(r_ref[...], wr_ref[...], preferred_element_type=F32)
    o_ref[...] = (ga_ref[...] * pa + gr_ref[...] * pr).astype(o_ref.dtype)


def _mm_merge(attn, rec, w_oa, w_or, gates, d, tm=1024, tn=512):
    m = attn.shape[0]
    tm = _tile(m, tm)
    tn = _tile(d, tn)
    n_j = d // tn
    return pl.pallas_call(
        _merge_kernel,
        grid=(m // tm, n_j),
        in_specs=[pl.BlockSpec((tm, attn.shape[1]), lambda i, j: (i, 0)),
                  pl.BlockSpec((tm, rec.shape[1]), lambda i, j: (i, 0)),
                  pl.BlockSpec((w_oa.shape[0], tn), lambda i, j: (0, j)),
                  pl.BlockSpec((w_or.shape[0], tn), lambda i, j: (0, j)),
                  pl.BlockSpec((tm, tn), lambda i, j: (i, j)),
                  pl.BlockSpec((tm, tn), lambda i, j: (i, n_j + j))],
        out_specs=pl.BlockSpec((tm, tn), lambda i, j: (i, j)),
        out_shape=jax.ShapeDtypeStruct((m, d), BF16),
        compiler_params=_cparams("parallel", "arbitrary"),
    )(attn, rec, w_oa, w_or, gates, gates)


def kernel(x_prompt, x_sample, p_prompt, p_sample, g_mix, w_in, rnn_conv_w, rnn_conv_b, lru_wa, lru_ba, lru_wx, lru_bx, lru_lambda, attn_sink, w_o_attn, w_o_rnn, w_merge_gate, w_out, g_ffn, w_up, ffn_conv_w, ffn_conv_b, w_down, w_ple_gate, w_ple, g_ple, g_final):
    wts = dict(g_mix=g_mix, w_in=w_in, rnn_conv_w=rnn_conv_w, rnn_conv_b=rnn_conv_b, lru_wa=lru_wa, lru_ba=lru_ba,
               lru_wx=lru_wx, lru_bx=lru_bx, lru_lambda=lru_lambda, attn_sink=attn_sink, w_o_attn=w_o_attn,
               w_o_rnn=w_o_rnn, w_merge_gate=w_merge_gate, w_out=w_out, g_ffn=g_ffn, w_up=w_up,
               ffn_conv_w=ffn_conv_w, ffn_conv_b=ffn_conv_b, w_down=w_down, w_ple_gate=w_ple_gate, w_ple=w_ple,
               g_ple=g_ple, g_final=g_final)
    return (_encoder(x_prompt, p_prompt, wts), _encoder(x_sample, p_sample, wts))
```

```python
import functools

import numpy as np
import jax
import jax.numpy as jnp
from jax import lax
from jax.experimental import pallas as pl
from jax.experimental.pallas import tpu as pltpu

F32 = jnp.float32
BF16 = jnp.bfloat16

EPS = 1e-6
NEG = -1e30
ATTN_BLOCK = 128
LRU_C = 8.0
RNN_HALO = 8
FFN_HALO = 16
LANES = 128
ACT_ROWS = 64
MXU_WIDTH = 256
V7X_VMEM_LIMIT_BYTES = 60 * 1024 * 1024


def _cparams(*sem, flags=None):
    return pltpu.CompilerParams(dimension_semantics=sem, vmem_limit_bytes=V7X_VMEM_LIMIT_BYTES, flags=flags)


def _tile(dim, pref):
    t = min(dim, pref)
    while dim % t:
        t //= 2
    return t


def _sigmoid(x):
    return 0.5 * jnp.tanh(0.5 * x) + 0.5


def _rms_scale(x, g):
    ms = jnp.mean(x * x, axis=-1, keepdims=True)
    return x * lax.rsqrt(ms + EPS) * g


def _rmsnorm_kernel(x_ref, g_ref, o_ref):
    o_ref[...] = _rms_scale(x_ref[...].astype(F32), g_ref[...]).astype(o_ref.dtype)


def _rmsnorm(x, g, out_dtype, tm=256):
    m, d = x.shape
    tm = _tile(m, tm)
    return pl.pallas_call(
        _rmsnorm_kernel,
        grid=(m // tm,),
        in_specs=[pl.BlockSpec((tm, d), lambda i: (i, 0)), pl.BlockSpec((1, d), lambda i: (0, 0))],
        out_specs=pl.BlockSpec((tm, d), lambda i: (i, 0)),
        out_shape=jax.ShapeDtypeStruct((m, d), out_dtype),
        compiler_params=_cparams("parallel"),
        name="rmsnorm",
    )(x, g.reshape(1, d).astype(F32))


def _mm_kernel(a_ref, b_ref, o_ref, *, epilogue):
    acc = jnp.dot(a_ref[...], b_ref[...], preferred_element_type=F32)
    o_ref[...] = epilogue(acc).astype(o_ref.dtype)


def _mm(a, b, epilogue, out_dtype, name, tm=1024, tn=1024):
    m, k = a.shape
    n = b.shape[1]
    tm = _tile(m, tm)
    tn = _tile(n, tn)
    return pl.pallas_call(
        functools.partial(_mm_kernel, epilogue=epilogue),
        grid=(m // tm, n // tn),
        in_specs=[pl.BlockSpec((tm, k), lambda i, j: (i, 0)), pl.BlockSpec((k, tn), lambda i, j: (0, j))],
        out_specs=pl.BlockSpec((tm, tn), lambda i, j: (i, j)),
        out_shape=jax.ShapeDtypeStruct((m, n), out_dtype),
        compiler_params=_cparams("parallel", "arbitrary"),
        name=name,
    )(a, b)


def _merge_kernel(a_ref, r_ref, wa_ref, wr_ref, ga_ref, gr_ref, o_ref):
    pa = jnp.dot(a_ref[...], wa_ref[...], preferred_element_type=F32)
    pr = jnp.dot(r_ref[...], wr_ref[...], preferred_element_type=F32)
    o_ref[...] = (ga_ref[...] * pa + gr_ref[...] * pr).astype(o_ref.dtype)


def _mm_merge(attn, rec, w_oa, w_or, gates, d, tm=1024, tn=512):
    m = attn.shape[0]
    tm = _tile(m, tm)
    tn = _tile(d, tn)
    n_j = d // tn
    return pl.pallas_call(
        _merge_kernel,
        grid=(m // tm, n_j),
        in_specs=[pl.BlockSpec((tm, attn.shape[1]), lambda i, j: (i, 0)),
                  pl.BlockSpec((tm, rec.shape[1]), lambda i, j: (i, 0)),
                  pl.BlockSpec((w_oa.shape[0], tn), lambda i, j: (0, j)),
                  pl.BlockSpec((w_or.shape[0], tn), lambda i, j: (0, j)),
                  pl.BlockSpec((tm, tn), lambda i, j: (i, j)),
                  pl.BlockSpec((tm, tn), lambda i, j: (i, n_j + j))],
        out_specs=pl.BlockSpec((tm, tn), lambda i, j: (i, j)),
        out_shape=jax.ShapeDtypeStruct((m, d), BF16),
        compiler_params=_cparams("parallel", "arbitrary"),
        name="merge",
    )(attn, rec, w_oa, w_or, gates, gates)


def _store_column_block(ref, j, n_j, tn, value):
    for jj in range(n_j):
        @pl.when(j == jj)
        def _(jj=jj):
            ref[:, jj * tn:(jj + 1) * tn] = value


def _out_norm_kernel(a_ref, w_ref, x_ref, g_ref, x1_ref, hn_ref, *, n_j, tn):
    j = pl.program_id(1)
    _store_column_block(x1_ref, j, n_j, tn,
                        x_ref[...] + jnp.dot(a_ref[...], w_ref[...], preferred_element_type=F32))

    @pl.when(j == n_j - 1)
    def _():
        hn_ref[...] = _rms_scale(x1_ref[...], g_ref[...]).astype(hn_ref.dtype)


def _out_norm(merged, w_out, x, g, tm=512, tn=512):
    m, d = x.shape
    tm = _tile(m, tm)
    tn = _tile(d, tn)
    n_j = d // tn
    row = pl.BlockSpec((tm, d), lambda i, j: (i, 0))
    return pl.pallas_call(
        functools.partial(_out_norm_kernel, n_j=n_j, tn=tn),
        grid=(m // tm, n_j),
        in_specs=[row, pl.BlockSpec((d, tn), lambda i, j: (0, j)), pl.BlockSpec((tm, tn), lambda i, j: (i, j)),
                  pl.BlockSpec((1, d), lambda i, j: (0, 0))],
        out_specs=[row, row],
        out_shape=[jax.ShapeDtypeStruct((m, d), F32), jax.ShapeDtypeStruct((m, d), BF16)],
        compiler_params=_cparams("parallel", "arbitrary"),
        name="out_norm",
    )(merged, w_out, x, g.reshape(1, d).astype(F32))


def _attn_kernel(sink_ref, q_ref, kp_ref, kc_ref, kn_ref, vp_ref, vc_ref, vn_ref, o_ref, bias_scr,
                 *, nb, n_kv, group, hd, slopes):
    blk = ATTN_BLOCK
    n = pl.program_id(1)

    @pl.when((pl.program_id(0) == 0) & (n == 0))
    def _():
        t = lax.broadcasted_iota(jnp.int32, (blk, 3 * blk), 0)
        s = lax.broadcasted_iota(jnp.int32, (blk, 3 * blk), 1)
        dist = jnp.abs(s - blk - t)
        distf = dist.astype(F32)
        for hq in range(n_kv * group):
            bias_scr[hq // group, (hq % group) * blk:(hq % group + 1) * blk, :] = jnp.where(
                dist <= blk, -slopes[hq] * distf, NEG)

    col = lax.broadcasted_iota(jnp.int32, (1, 3 * blk), 1)
    edge = jnp.where(((col < blk) & (n == 0)) | ((col >= 2 * blk) & (n == nb - 1)), NEG, 0.0).astype(F32)
    scale = hd ** -0.5

    for h in range(n_kv):
        q_h = jnp.concatenate(
            [q_ref[:, (h * group + g) * hd:(h * group + g + 1) * hd] for g in range(group)], axis=0)
        k_h = jnp.concatenate([r[:, h * hd:(h + 1) * hd] for r in (kp_ref, kc_ref, kn_ref)], axis=0)
        v_h = jnp.concatenate([r[:, h * hd:(h + 1) * hd] for r in (vp_ref, vc_ref, vn_ref)], axis=0)
        s = lax.dot_general(q_h, k_h, (((1,), (1,)), ((), ())), preferred_element_type=F32)
        s = s * scale + bias_scr[h] + edge
        sk = jnp.concatenate(
            [jnp.full((blk, 1), sink_ref[h * group + g], F32) for g in range(group)], axis=0)
        m = jnp.maximum(jnp.max(s, axis=-1, keepdims=True), sk)
        e = jnp.exp(s - m)
        denom = jnp.sum(e, axis=-1, keepdims=True) + jnp.exp(sk - m)
        o = jnp.dot(e.astype(BF16), v_h, preferred_element_type=F32) * (1.0 / denom)
        for g in range(group):
            o_ref[:, (h * group + g) * hd:(h * group + g + 1) * hd] = o[g * blk:(g + 1) * blk].astype(o_ref.dtype)


def _attention(qkv, sink, batch, seq, n_q, n_kv, hd):
    blk = ATTN_BLOCK
    nb = seq // blk
    attn_w, kv_w = n_q * hd, n_kv * hd
    assert attn_w % kv_w == 0 and seq % blk == 0
    kcol = attn_w // kv_w
    slopes = tuple(float(v) for v in (2.0 ** (-8.0 * (np.arange(n_q) + 1) / n_q)).astype(np.float32))

    def spec(w, colblk, shift):
        return pl.BlockSpec((blk, w), lambda b, n: (b * nb + jnp.clip(n + shift, 0, nb - 1), colblk))

    kern = functools.partial(_attn_kernel, nb=nb, n_kv=n_kv, group=n_q // n_kv, hd=hd, slopes=slopes)
    return pl.pallas_call(
        kern,
        grid=(batch, nb),
        in_specs=[pl.BlockSpec(memory_space=pltpu.SMEM),
                  spec(attn_w, 0, 0),
                  spec(kv_w, kcol, -1), spec(kv_w, kcol, 0), spec(kv_w, kcol, 1),
                  spec(kv_w, kcol + 1, -1), spec(kv_w, kcol + 1, 0), spec(kv_w, kcol + 1, 1)],
        out_specs=pl.BlockSpec((blk, attn_w), lambda b, n: (b * nb + n, 0)),
        out_shape=jax.ShapeDtypeStruct((batch * seq, attn_w), BF16),
        scratch_shapes=[pltpu.VMEM((n_kv, (n_q // n_kv) * blk, 3 * blk), F32)],
        compiler_params=_cparams("arbitrary", "arbitrary"),
        name="banded_gqa",
    )(sink.astype(F32), qkv, qkv, qkv, qkv, qkv, qkv, qkv)


def _lru_scan(a_scr, b_scr, carry_scr, n_slab, reverse):
    row = lax.broadcasted_iota(jnp.int32, a_scr.shape[1:], 0)

    def slab(i, carry):
        idx = (n_slab - 1 - i) if reverse else i
        a = a_scr[idx]
        b = b_scr[idx]
        for k in (1, 2, 4):
            keep = (row < 8 - k) if reverse else (row >= k)
            shift = (8 - k) if reverse else k
            a_sh = jnp.where(keep, pltpu.roll(a, shift, 0), 1.0)
            b_sh = jnp.where(keep, pltpu.roll(b, shift, 0), 0.0)
            b = b + a * b_sh
            a = a * a_sh
        h = a * carry + b
        b_scr[idx] = h
        return jnp.broadcast_to(h[0:1] if reverse else h[7:8], h.shape)

    carry_scr[...] = lax.fori_loop(0, n_slab, slab, carry_scr[...], unroll=4)


def _rnn_kernel(*refs, reverse, n_tt, tt_rows, n_sub, sub_w):
    if reverse:
        (xc_ref, wa_ref, wx_ref, ba_ref, bx_ref, lam_ref, hf_ref, gy_ref, o_ref, a_scr, b_scr, carry_scr) = refs
    else:
        (xm_ref, xp_ref, xn_ref, cw_ref, cb_ref, wa_ref, wx_ref, ba_ref, bx_ref, lam_ref,
         o_ref, xc_ref, a_scr, b_scr, carry_scr) = refs
    halo = RNN_HALO
    tt = pl.program_id(2)

    @pl.when(tt == 0)
    def _():
        carry_scr[...] = jnp.zeros_like(carry_scr)

    if reverse:
        xc = xc_ref[...]
    else:
        xp = jnp.where(tt > 0, xp_ref[...], 0.0)
        xn = jnp.where(tt < n_tt - 1, xn_ref[...], 0.0)
        ext = jnp.concatenate([xp, xm_ref[...], xn], axis=0)
        n_ext = tt_rows + 2 * halo
        cw = cw_ref[...]
        xc = pltpu.roll(ext, 1, 0)[halo:halo + tt_rows] * cw[0:1]
        xc = xc + ext[halo:halo + tt_rows] * cw[1:2]
        xc = xc + pltpu.roll(ext, n_ext - 1, 0)[halo:halo + tt_rows] * cw[2:3]
        xc = xc + pltpu.roll(ext, n_ext - 2, 0)[halo:halo + tt_rows] * cw[3:4]
        xc = xc + cb_ref[...]
        xc_ref[...] = xc

    xcb = xc.astype(BF16)
    r_pre = jnp.concatenate(
        [jnp.dot(xcb[:, s * sub_w:(s + 1) * sub_w], wa_ref[s], preferred_element_type=F32) for s in range(n_sub)],
        axis=1)
    i_pre = jnp.concatenate(
        [jnp.dot(xcb[:, s * sub_w:(s + 1) * sub_w], wx_ref[s], preferred_element_type=F32) for s in range(n_sub)],
        axis=1)
    r = _sigmoid(r_pre + ba_ref[...])
    ig = _sigmoid(i_pre + bx_ref[...])
    log_a = (-LRU_C) * r * jax.nn.softplus(-lam_ref[...])
    a = jnp.exp(log_a)
    one_minus_a2 = jnp.tanh(-log_a) * (1.0 + a * a)
    a_scr[...] = a.reshape(a_scr.shape)
    b_scr[...] = (jnp.sqrt(one_minus_a2) * (ig * xc)).reshape(b_scr.shape)

    _lru_scan(a_scr, b_scr, carry_scr, tt_rows // 8, reverse)
    hs = b_scr[...].reshape(tt_rows, -1)
    if reverse:
        o_ref[...] = ((hf_ref[...] + hs) * gy_ref[...]).astype(o_ref.dtype)
    else:
        o_ref[...] = hs


def _rnn_direction(d, x_in, conv_w, conv_b, wa, wx, ba, bx, lam, batch, seq, hf=None, gy=None, tt_rows=512, cb=512):
    m, d_rnn = x_in.shape
    sub_w = wa.shape[2]
    reverse = d == 1
    halo = RNN_HALO
    tt_rows = _tile(seq, tt_rows)
    cb = max(_tile(d_rnn, cb), sub_w)
    n_sub = cb // sub_w
    n_tt = seq // tt_rows
    hb = tt_rows // halo
    n_hb = seq // halo

    def te(t):
        return (n_tt - 1 - t) if reverse else t

    main = pl.BlockSpec((tt_rows, cb), lambda b, c, t: (b * n_tt + te(t), c))
    vec = lambda rows: pl.BlockSpec((rows, cb), lambda b, c, t: (0, c))
    wspec = pl.BlockSpec((n_sub, sub_w, sub_w), lambda b, c, t: (c, 0, 0))
    gate_specs = [wspec, wspec, vec(1), vec(1), vec(1)]
    gate_args = [wa[d].astype(BF16), wx[d].astype(BF16), ba[d].reshape(1, -1).astype(F32),
                 bx[d].reshape(1, -1).astype(F32), lam[d].reshape(1, -1).astype(F32)]
    if reverse:
        in_specs = [main] + gate_specs + [main, main]
        args = [x_in] + gate_args + [hf, gy]
        out_specs = main
        out_shape = jax.ShapeDtypeStruct((m, d_rnn), BF16)
    else:
        prev = pl.BlockSpec((halo, cb), lambda b, c, t: (b * n_hb + jnp.maximum(t * hb - 1, 0), c))
        nxt = pl.BlockSpec((halo, cb), lambda b, c, t: (b * n_hb + jnp.minimum((t + 1) * hb, n_hb - 1), c))
        in_specs = [main, prev, nxt, vec(4), vec(1)] + gate_specs
        args = [x_in, x_in, x_in, conv_w.astype(F32), conv_b.reshape(1, -1).astype(F32)] + gate_args
        out_specs = [main, main]
        out_shape = [jax.ShapeDtypeStruct((m, d_rnn), F32), jax.ShapeDtypeStruct((m, d_rnn), F32)]
    kern = functools.partial(_rnn_kernel, reverse=reverse, n_tt=n_tt, tt_rows=tt_rows, n_sub=n_sub, sub_w=sub_w)
    return pl.pallas_call(
        kern,
        grid=(batch, d_rnn // cb, n_tt),
        in_specs=in_specs,
        out_specs=out_specs,
        out_shape=out_shape,
        scratch_shapes=[pltpu.VMEM((tt_rows // 8, 8, cb), F32), pltpu.VMEM((tt_rows // 8, 8, cb), F32),
                        pltpu.VMEM((8, cb), F32)],
        compiler_params=_cparams("arbitrary", "arbitrary", "arbitrary"),
        name="rglru_bwd" if reverse else "rglru_fwd",
    )(*args)


def _ffn_kernel(hm_ref, hp_ref, hn_ref, x1_ref, wv_ref, wg_ref, cwv_ref, cwg_ref, cbv_ref, cbg_ref, wd_ref,
                x2_ref, x2b_ref, ext_scr, u_scr, act_scr, *, n_tt, n_fc, tm, sub, nsplit):
    halo = FFN_HALO
    tt = pl.program_id(1)
    j = pl.program_id(2)
    fc = act_scr.shape[2]
    n_sub = fc // sub
    n_lt = sub // LANES
    d = x2_ref.shape[1]

    def up_piece(slot, s, k):
        def run():
            w_ref = (wv_ref, wg_ref)[k]
            u = jnp.dot(ext_scr[...], w_ref[:, s * sub:(s + 1) * sub], preferred_element_type=F32)
            for c in range(n_lt):
                u_scr[slot, k, s * n_lt + c] = u[:, c * LANES:(c + 1) * LANES]
        return run

    def conv(slot, k, c, r0, cw_ref, cb_ref):
        cols = slice(c * LANES, (c + 1) * LANES)
        out = u_scr[slot, k, c, pl.ds(halo - 1 + r0, ACT_ROWS), :] * cw_ref[0:1, cols]
        out = out + u_scr[slot, k, c, pl.ds(halo + r0, ACT_ROWS), :] * cw_ref[1:2, cols]
        out = out + u_scr[slot, k, c, pl.ds(halo + 1 + r0, ACT_ROWS), :] * cw_ref[2:3, cols]
        return out + cb_ref[:, cols]

    def act_piece(slot, c):
        def run():
            for r0 in range(0, tm, ACT_ROWS):
                val = conv(slot, 0, c, r0, cwv_ref, cbv_ref)
                gate = conv(slot, 1, c, r0, cwg_ref, cbg_ref)
                act_scr[slot, r0:r0 + ACT_ROWS, c * LANES:(c + 1) * LANES] = (jax.nn.gelu(gate) * val).astype(BF16)
        return run

    def down_piece(slot, n):
        def run():
            ncols = slice(n * nsplit, (n + 1) * nsplit)
            x2_ref[:, ncols] += jnp.dot(act_scr[slot], wd_ref[:, ncols], preferred_element_type=F32)
        return run

    def up(slot):
        return [up_piece(slot, s, k) for s in range(n_sub) for k in range(2)]

    def activate(slot):
        return [act_piece(slot, c) for c in range(fc // LANES)]

    def down(slot):
        return [down_piece(slot, n) for n in range(d // nsplit)]

    def run_interleaved(mxu_pieces, vpu_pieces):
        n_m, n_v = len(mxu_pieces), len(vpu_pieces)
        done_v = 0
        for i, piece in enumerate(mxu_pieces):
            piece()
            upto = (i + 1) * n_v // n_m if n_m else n_v
            for v in vpu_pieces[done_v:upto]:
                v()
            done_v = max(done_v, upto)
        for v in vpu_pieces[done_v:]:
            v()

    @pl.when(j == 0)
    def _():
        ext_scr[0:halo, :] = jnp.where(tt > 0, hp_ref[...], jnp.zeros_like(hp_ref))
        ext_scr[halo:halo + tm, :] = hm_ref[...]
        ext_scr[halo + tm:, :] = jnp.where(tt < n_tt - 1, hn_ref[...], jnp.zeros_like(hn_ref))
        x2_ref[...] = x1_ref[...]
        run_interleaved(up(0), [])

    @pl.when(j == 1)
    def _():
        run_interleaved(up(1), activate(0))

    for par in range(2):
        @pl.when((j >= 2) & (j < n_fc) & (lax.rem(j, 2) == par))
        def _(par=par):
            run_interleaved(up(par), activate(1 - par))
            run_interleaved(down(par), [])

    @pl.when(j == n_fc)
    def _():
        run_interleaved(down(n_fc % 2), activate((n_fc - 1) % 2))

    @pl.when(j == n_fc + 1)
    def _():
        run_interleaved(down((n_fc - 1) % 2), [])
        x2b_ref[...] = x2_ref[...].astype(BF16)


def _ffn(hn, x1, w_up, conv_w, conv_b, w_down, batch, seq, tm=512, fc=512):
    m, d = hn.shape
    d_ff = w_down.shape[0]
    halo = FFN_HALO
    tm = _tile(seq, tm)
    fc = _tile(d_ff, fc)
    sub = _tile(fc, MXU_WIDTH)
    nsplit = _tile(d, 512)
    n_tt = seq // tm
    n_fc = d_ff // fc
    hb = tm // halo
    n_hb = seq // halo
    assert n_fc >= 2
    main = pl.BlockSpec((tm, d), lambda b, t, j: (b * n_tt + t, 0), pipeline_mode=pl.Buffered(1))
    prev = pl.BlockSpec((halo, d), lambda b, t, j: (b * n_hb + jnp.maximum(t * hb - 1, 0), 0))
    nxt = pl.BlockSpec((halo, d), lambda b, t, j: (b * n_hb + jnp.minimum((t + 1) * hb, n_hb - 1), 0))
    chunk = lambda j, lag: jnp.clip(j - lag, 0, n_fc - 1)
    wv = pl.BlockSpec((d, fc), lambda b, t, j: (0, chunk(j, 0)))
    wg = pl.BlockSpec((d, fc), lambda b, t, j: (0, n_fc + chunk(j, 0)))
    cv = lambda rows: pl.BlockSpec((rows, fc), lambda b, t, j: (0, chunk(j, 1)))
    cg = lambda rows: pl.BlockSpec((rows, fc), lambda b, t, j: (0, n_fc + chunk(j, 1)))
    wd = pl.BlockSpec((fc, d), lambda b, t, j: (chunk(j, 2), 0))
    cb2 = conv_b.reshape(1, -1).astype(F32)
    cw = conv_w.astype(F32)
    kern = functools.partial(_ffn_kernel, n_tt=n_tt, n_fc=n_fc, tm=tm, sub=sub, nsplit=nsplit)
    return pl.pallas_call(
        kern,
        grid=(batch, n_tt, n_fc + 2),
        in_specs=[main, prev, nxt, main, wv, wg, cv(3), cg(3), cv(1), cg(1), wd],
        out_specs=[main, main],
        out_shape=[jax.ShapeDtypeStruct((m, d), F32), jax.ShapeDtypeStruct((m, d), BF16)],
        scratch_shapes=[pltpu.VMEM((tm + 2 * halo, d), BF16),
                        pltpu.VMEM((2, 2, fc // LANES, tm + 2 * halo, LANES), F32),
                        pltpu.VMEM((2, tm, fc), BF16)],
        compiler_params=_cparams("parallel", "arbitrary", "arbitrary"),
        name="conv_mlp",
    )(hn, hn, hn, x1, w_up, w_up, cw, cw, cb2, cb2, w_down)


def _ple_kernel(xb_ref, x2_ref, p_ref, wp_ref, gp_ref, wg_ref, gf_ref, o_ref, e_scr, *, n_j, tn):
    j = pl.program_id(1)
    d = o_ref.shape[1]

    @pl.when(j == 0)
    def _():
        pb = p_ref[...].astype(BF16)
        ssq = jnp.zeros((pb.shape[0], 1), F32)
        for c in range(n_j):
            pe = jnp.dot(pb, wp_ref[:, c * tn:(c + 1) * tn], preferred_element_type=F32)
            e_scr[:, c * tn:(c + 1) * tn] = pe
            ssq = ssq + jnp.sum(pe * pe, axis=-1, keepdims=True)
        e_scr[...] = e_scr[...] * lax.rsqrt(ssq * (1.0 / d) + EPS) * gp_ref[...]

    gate = _sigmoid(jnp.dot(xb_ref[...], wg_ref[...], preferred_element_type=F32))
    for jj in range(n_j):
        @pl.when(j == jj)
        def _(jj=jj):
            o_ref[:, jj * tn:(jj + 1) * tn] = x2_ref[...] + gate * e_scr[:, jj * tn:(jj + 1) * tn]

    @pl.when(j == n_j - 1)
    def _():
        o_ref[...] = _rms_scale(o_ref[...], gf_ref[...])


def _ple_final(x2b, x2, p, w_ple, g_ple, w_gate, g_final, tm=512, tn=512):
    m, d = x2.shape
    kp = p.shape[1]
    tm = _tile(m, tm)
    tn = _tile(d, tn)
    n_j = d // tn
    row = lambda w: pl.BlockSpec((tm, w), lambda i, j: (i, 0))
    full = lambda r: pl.BlockSpec((r, d), lambda i, j: (0, 0))
    return pl.pallas_call(
        functools.partial(_ple_kernel, n_j=n_j, tn=tn),
        grid=(m // tm, n_j),
        in_specs=[row(d), pl.BlockSpec((tm, tn), lambda i, j: (i, j)), row(kp), full(kp), full(1),
                  pl.BlockSpec((d, tn), lambda i, j: (0, j)), full(1)],
        out_specs=row(d),
        out_shape=jax.ShapeDtypeStruct((m, d), F32),
        scratch_shapes=[pltpu.VMEM((tm, d), F32)],
        compiler_params=_cparams("parallel", "arbitrary"),
        name="ple_final",
    )(x2b, x2, p, w_ple, g_ple.reshape(1, d).astype(F32), w_gate, g_final.reshape(1, d).astype(F32))


def _encoder(x3, p4, wts):
    batch, seq, d = x3.shape
    m = batch * seq
    x = x3.reshape(m, d)
    n_q = wts["attn_sink"].shape[-1]
    attn_w = wts["w_o_attn"].shape[-2]
    d_rnn = wts["w_o_rnn"].shape[-2]
    hd = attn_w // n_q
    kv_w = (wts["w_in"].shape[-1] - attn_w - 2 * d_rnn) // 2
    n_kv = kv_w // hd
    qkv_w = attn_w + 2 * kv_w
    depth = wts["w_in"].shape[0]
    assert depth == 1, "the final norm is fused into the (single) layer's embedding kernel"
    ident = lambda acc: acc

    for l in range(depth):
        w_in = wts["w_in"][l]
        h = _rmsnorm(x, wts["g_mix"][l], BF16)
        qkv = _mm(h, w_in[:, :qkv_w].astype(BF16), ident, BF16, "in_qkv")
        xr = _mm(h, w_in[:, qkv_w:qkv_w + d_rnn].astype(BF16), ident, F32, "in_xr")
        gy = _mm(h, w_in[:, qkv_w + d_rnn:].astype(BF16), jax.nn.gelu, F32, "in_gelu_y")
        gates = _mm(h, wts["w_merge_gate"][l].astype(BF16), _sigmoid, F32, "merge_gates")

        attn = _attention(qkv, wts["attn_sink"][l], batch, seq, n_q, n_kv, hd)

        lru = (wts["rnn_conv_w"][l], wts["rnn_conv_b"][l], wts["lru_wa"][l], wts["lru_wx"][l],
               wts["lru_ba"][l], wts["lru_bx"][l], wts["lru_lambda"][l], batch, seq)
        h_fwd, xc = _rnn_direction(0, xr, *lru)
        rec = _rnn_direction(1, xc, *lru, hf=h_fwd, gy=gy)

        merged = _mm_merge(attn, rec, wts["w_o_attn"][l].astype(BF16), wts["w_o_rnn"][l].astype(BF16), gates, d)
        x1, hn = _out_norm(merged, wts["w_out"][l].astype(BF16), x, wts["g_ffn"][l])
        x2, x2b = _ffn(hn, x1, wts["w_up"][l].astype(BF16), wts["ffn_conv_w"][l], wts["ffn_conv_b"][l],
                       wts["w_down"][l].astype(BF16), batch, seq)
        y = _ple_final(x2b, x2, p4[l].reshape(m, -1), wts["w_ple"][l].astype(BF16), wts["g_ple"][l],
                       wts["w_ple_gate"][l].astype(BF16), wts["g_final"])
    return y.reshape(batch, seq, d)


def kernel(x_prompt, x_sample, p_prompt, p_sample, g_mix, w_in, rnn_conv_w, rnn_conv_b, lru_wa, lru_ba, lru_wx, lru_bx, lru_lambda, attn_sink, w_o_attn, w_o_rnn, w_merge_gate, w_out, g_ffn, w_up, ffn_conv_w, ffn_conv_b, w_down, w_ple_gate, w_ple, g_ple, g_final):
    wts = dict(g_mix=g_mix, w_in=w_in, rnn_conv_w=rnn_conv_w, rnn_conv_b=rnn_conv_b, lru_wa=lru_wa, lru_ba=lru_ba,
               lru_wx=lru_wx, lru_bx=lru_bx, lru_lambda=lru_lambda, attn_sink=attn_sink, w_o_attn=w_o_attn,
               w_o_rnn=w_o_rnn, w_merge_gate=w_merge_gate, w_out=w_out, g_ffn=g_ffn, w_up=w_up,
               ffn_conv_w=ffn_conv_w, ffn_conv_b=ffn_conv_b, w_down=w_down, w_ple_gate=w_ple_gate, w_ple=w_ple,
               g_ple=g_ple, g_final=g_final)
    return (_encoder(x_prompt, p_prompt, wts), _encoder(x_sample, p_sample, wts))
```

```python
import functools

import numpy as np
import jax
import jax.numpy as jnp
from jax import lax
from jax.experimental import pallas as pl
from jax.experimental.pallas import tpu as pltpu

F32 = jnp.float32
BF16 = jnp.bfloat16

EPS = 1e-6
NEG = -1e30
ATTN_BLOCK = 128
LRU_C = 8.0
RNN_HALO = 8
FFN_HALO = 16
LANES = 128
ACT_ROWS = 64
MXU_WIDTH = 256
V7X_VMEM_LIMIT_BYTES = 60 * 1024 * 1024


def _cparams(*sem, flags=None):
    return pltpu.CompilerParams(dimension_semantics=sem, vmem_limit_bytes=V7X_VMEM_LIMIT_BYTES, flags=flags)


def _tile(dim, pref):
    t = min(dim, pref)
    while dim % t:
        t //= 2
    return t


def _chunk_major(w, tn):
    k, n = w.shape
    return w.reshape(k, n // tn, tn).transpose(1, 0, 2)


def _sigmoid(x):
    return 0.5 * jnp.tanh(0.5 * x) + 0.5


def _rms_scale(x, g):
    ms = jnp.mean(x * x, axis=-1, keepdims=True)
    return x * lax.rsqrt(ms + EPS) * g


def _rmsnorm_kernel(x_ref, g_ref, o_ref):
    o_ref[...] = _rms_scale(x_ref[...].astype(F32), g_ref[...]).astype(o_ref.dtype)


def _rmsnorm(x, g, out_dtype, tm=256):
    m, d = x.shape
    tm = _tile(m, tm)
    return pl.pallas_call(
        _rmsnorm_kernel,
        grid=(m // tm,),
        in_specs=[pl.BlockSpec((tm, d), lambda i: (i, 0)), pl.BlockSpec((1, d), lambda i: (0, 0))],
        out_specs=pl.BlockSpec((tm, d), lambda i: (i, 0)),
        out_shape=jax.ShapeDtypeStruct((m, d), out_dtype),
        compiler_params=_cparams("parallel"),
        name="rmsnorm",
    )(x, g.reshape(1, d).astype(F32))


def _mm_kernel(a_ref, b_ref, o_ref, *, epilogue):
    acc = jnp.dot(a_ref[...], b_ref[...], preferred_element_type=F32)
    o_ref[...] = epilogue(acc).astype(o_ref.dtype)


def _mm(a, b, epilogue, out_dtype, name, tm=1024, tn=1024):
    m, k = a.shape
    n = b.shape[1]
    tm = _tile(m, tm)
    tn = _tile(n, tn)
    return pl.pallas_call(
        functools.partial(_mm_kernel, epilogue=epilogue),
        grid=(m // tm, n // tn),
        in_specs=[pl.BlockSpec((tm, k), lambda i, j: (i, 0)), pl.BlockSpec((k, tn), lambda i, j: (0, j))],
        out_specs=pl.BlockSpec((tm, tn), lambda i, j: (i, j)),
        out_shape=jax.ShapeDtypeStruct((m, n), out_dtype),
        compiler_params=_cparams("parallel", "arbitrary"),
        name=name,
    )(a, b)


def _merge_kernel(a_ref, r_ref, wa_ref, wr_ref, ga_ref, gr_ref, o_ref):
    pa = jnp.dot(a_ref[...], wa_ref[...], preferred_element_type=F32)
    pr = jnp.dot(r_ref[...], wr_ref[...], preferred_element_type=F32)
    o_ref[...] = (ga_ref[...] * pa + gr_ref[...] * pr).astype(o_ref.dtype)


def _mm_merge(attn, rec, w_oa, w_or, gates, d, tm=1024, tn=512):
    m = attn.shape[0]
    tm = _tile(m, tm)
    tn = _tile(d, tn)
    n_j = d // tn
    return pl.pallas_call(
        _merge_kernel,
        grid=(m // tm, n_j),
        in_specs=[pl.BlockSpec((tm, attn.shape[1]), lambda i, j: (i, 0)),
                  pl.BlockSpec((tm, rec.shape[1]), lambda i, j: (i, 0)),
                  pl.BlockSpec((w_oa.shape[0], tn), lambda i, j: (0, j)),
                  pl.BlockSpec((w_or.shape[0], tn), lambda i, j: (0, j)),
                  pl.BlockSpec((tm, tn), lambda i, j: (i, j)),
                  pl.BlockSpec((tm, tn), lambda i, j: (i, n_j + j))],
        out_specs=pl.BlockSpec((tm, tn), lambda i, j: (i, j)),
        out_shape=jax.ShapeDtypeStruct((m, d), BF16),
        compiler_params=_cparams("parallel", "arbitrary"),
        name="merge",
    )(attn, rec, w_oa, w_or, gates, gates)


def _store_column_block(ref, j, n_j, tn, value):
    for jj in range(n_j):
        @pl.when(j == jj)
        def _(jj=jj):
            ref[:, jj * tn:(jj + 1) * tn] = value


def _out_norm_kernel(a_ref, w_ref, x_ref, g_ref, x1_ref, hn_ref, *, n_j, tn):
    j = pl.program_id(1)
    _store_column_block(x1_ref, j, n_j, tn,
                        x_ref[...] + jnp.dot(a_ref[...], w_ref[...], preferred_element_type=F32))

    @pl.when(j == n_j - 1)
    def _():
        hn_ref[...] = _rms_scale(x1_ref[...], g_ref[...]).astype(hn_ref.dtype)


def _out_norm(merged, w_out, x, g, tm=512, tn=512):
    m, d = x.shape
    tm = _tile(m, tm)
    tn = _tile(d, tn)
    n_j = d // tn
    row = pl.BlockSpec((tm, d), lambda i, j: (i, 0))
    return pl.pallas_call(
        functools.partial(_out_norm_kernel, n_j=n_j, tn=tn),
        grid=(m // tm, n_j),
        in_specs=[row, pl.BlockSpec((None, d, tn), lambda i, j: (j, 0, 0)),
                  pl.BlockSpec((tm, tn), lambda i, j: (i, j)), pl.BlockSpec((1, d), lambda i, j: (0, 0))],
        out_specs=[row, row],
        out_shape=[jax.ShapeDtypeStruct((m, d), F32), jax.ShapeDtypeStruct((m, d), BF16)],
        compiler_params=_cparams("parallel", "arbitrary"),
        name="out_norm",
    )(merged, _chunk_major(w_out, tn), x, g.reshape(1, d).astype(F32))


def _attn_kernel(sink_ref, q_ref, kp_ref, kc_ref, kn_ref, vp_ref, vc_ref, vn_ref, o_ref, bias_scr,
                 *, nb, n_kv, group, hd, slopes):
    blk = ATTN_BLOCK
    n = pl.program_id(1)

    @pl.when((pl.program_id(0) == 0) & (n == 0))
    def _():
        t = lax.broadcasted_iota(jnp.int32, (blk, 3 * blk), 0)
        s = lax.broadcasted_iota(jnp.int32, (blk, 3 * blk), 1)
        dist = jnp.abs(s - blk - t)
        distf = dist.astype(F32)
        for hq in range(n_kv * group):
            bias_scr[hq * blk:(hq + 1) * blk, :] = jnp.where(dist <= blk, -slopes[hq] * distf, NEG)

    col = lax.broadcasted_iota(jnp.int32, (1, 3 * blk), 1)
    edge = jnp.where(((col < blk) & (n == 0)) | ((col >= 2 * blk) & (n == nb - 1)), NEG, 0.0).astype(F32)
    scale = hd ** -0.5
    n_q = n_kv * group

    q_rows = [q_ref[:, hq * hd:(hq + 1) * hd] for hq in range(n_q)]
    scores = []
    for h in range(n_kv):
        q_h = jnp.concatenate(q_rows[h * group:(h + 1) * group], axis=0)
        k_h = jnp.concatenate([r[:, h * hd:(h + 1) * hd] for r in (kp_ref, kc_ref, kn_ref)], axis=0)
        scores.append(lax.dot_general(q_h, k_h, (((1,), (1,)), ((), ())), preferred_element_type=F32))
    s = jnp.concatenate(scores, axis=0) * scale + bias_scr[...] + edge
    sk = jnp.concatenate([jnp.full((blk, 1), sink_ref[hq], F32) for hq in range(n_q)], axis=0)
    m = jnp.maximum(jnp.max(s, axis=-1, keepdims=True), sk)
    e = jnp.exp(s - m)
    inv = 1.0 / (jnp.sum(e, axis=-1, keepdims=True) + jnp.exp(sk - m))
    eb = e.astype(BF16)
    rows = group * blk
    for h in range(n_kv):
        v_h = jnp.concatenate([r[:, h * hd:(h + 1) * hd] for r in (vp_ref, vc_ref, vn_ref)], axis=0)
        o = jnp.dot(eb[h * rows:(h + 1) * rows], v_h, preferred_element_type=F32) * inv[h * rows:(h + 1) * rows]
        for g in range(group):
            o_ref[:, (h * group + g) * hd:(h * group + g + 1) * hd] = o[g * blk:(g + 1) * blk].astype(o_ref.dtype)


def _attention(qkv, sink, batch, seq, n_q, n_kv, hd):
    blk = ATTN_BLOCK
    nb = seq // blk
    attn_w, kv_w = n_q * hd, n_kv * hd
    assert attn_w % kv_w == 0 and seq % blk == 0
    kcol = attn_w // kv_w
    slopes = tuple(float(v) for v in (2.0 ** (-8.0 * (np.arange(n_q) + 1) / n_q)).astype(np.float32))

    def spec(w, colblk, shift):
        return pl.BlockSpec((blk, w), lambda b, n: (b * nb + jnp.clip(n + shift, 0, nb - 1), colblk))

    kern = functools.partial(_attn_kernel, nb=nb, n_kv=n_kv, group=n_q // n_kv, hd=hd, slopes=slopes)
    return pl.pallas_call(
        kern,
        grid=(batch, nb),
        in_specs=[pl.BlockSpec(memory_space=pltpu.SMEM),
                  spec(attn_w, 0, 0),
                  spec(kv_w, kcol, -1), spec(kv_w, kcol, 0), spec(kv_w, kcol, 1),
                  spec(kv_w, kcol + 1, -1), spec(kv_w, kcol + 1, 0), spec(kv_w, kcol + 1, 1)],
        out_specs=pl.BlockSpec((blk, attn_w), lambda b, n: (b * nb + n, 0)),
        out_shape=jax.ShapeDtypeStruct((batch * seq, attn_w), BF16),
        scratch_shapes=[pltpu.VMEM((n_q * blk, 3 * blk), F32)],
        compiler_params=_cparams("arbitrary", "arbitrary"),
        name="banded_gqa",
    )(sink.astype(F32), qkv, qkv, qkv, qkv, qkv, qkv, qkv)


def _lru_scan(a_scr, b_scr, carry_scr, n_slab, reverse):
    row = lax.broadcasted_iota(jnp.int32, a_scr.shape[1:], 0)

    def slab(i, carry):
        idx = (n_slab - 1 - i) if reverse else i
        a = a_scr[idx]
        b = b_scr[idx]
        for k in (1, 2, 4):
            keep = (row < 8 - k) if reverse else (row >= k)
            shift = (8 - k) if reverse else k
            a_sh = jnp.where(keep, pltpu.roll(a, shift, 0), 1.0)
            b_sh = jnp.where(keep, pltpu.roll(b, shift, 0), 0.0)
            b = b + a * b_sh
            a = a * a_sh
        h = a * carry + b
        b_scr[idx] = h
        return jnp.broadcast_to(h[0:1] if reverse else h[7:8], h.shape)

    carry_scr[...] = lax.fori_loop(0, n_slab, slab, carry_scr[...], unroll=4)


def _rnn_kernel(*refs, reverse, n_tt, tt_rows, n_sub, sub_w):
    if reverse:
        (xc_ref, wa_ref, wx_ref, ba_ref, bx_ref, lam_ref, hf_ref, gy_ref, o_ref, a_scr, b_scr, carry_scr) = refs
    else:
        (xm_ref, xp_ref, xn_ref, cw_ref, cb_ref, wa_ref, wx_ref, ba_ref, bx_ref, lam_ref,
         o_ref, xc_ref, a_scr, b_scr, carry_scr) = refs
    halo = RNN_HALO
    tt = pl.program_id(2)

    @pl.when(tt == 0)
    def _():
        carry_scr[...] = jnp.zeros_like(carry_scr)

    if reverse:
        xc = xc_ref[...]
    else:
        xp = jnp.where(tt > 0, xp_ref[...], 0.0)
        xn = jnp.where(tt < n_tt - 1, xn_ref[...], 0.0)
        ext = jnp.concatenate([xp, xm_ref[...], xn], axis=0)
        n_ext = tt_rows + 2 * halo
        cw = cw_ref[...]
        xc = pltpu.roll(ext, 1, 0)[halo:halo + tt_rows] * cw[0:1]
        xc = xc + ext[halo:halo + tt_rows] * cw[1:2]
        xc = xc + pltpu.roll(ext, n_ext - 1, 0)[halo:halo + tt_rows] * cw[2:3]
        xc = xc + pltpu.roll(ext, n_ext - 2, 0)[halo:halo + tt_rows] * cw[3:4]
        xc = xc + cb_ref[...]
        xc_ref[...] = xc

    xcb = xc.astype(BF16)
    r_pre = jnp.concatenate(
        [jnp.dot(xcb[:, s * sub_w:(s + 1) * sub_w], wa_ref[s], preferred_element_type=F32) for s in range(n_sub)],
        axis=1)
    i_pre = jnp.concatenate(
        [jnp.dot(xcb[:, s * sub_w:(s + 1) * sub_w], wx_ref[s], preferred_element_type=F32) for s in range(n_sub)],
        axis=1)
    r = _sigmoid(r_pre + ba_ref[...])
    ig = _sigmoid(i_pre + bx_ref[...])
    log_a = (-LRU_C) * r * jax.nn.softplus(-lam_ref[...])
    a = jnp.exp(log_a)
    one_minus_a2 = jnp.tanh(-log_a) * (1.0 + a * a)
    a_scr[...] = a.reshape(a_scr.shape)
    b_scr[...] = (jnp.sqrt(one_minus_a2) * (ig * xc)).reshape(b_scr.shape)

    _lru_scan(a_scr, b_scr, carry_scr, tt_rows // 8, reverse)
    hs = b_scr[...].reshape(tt_rows, -1)
    if reverse:
        o_ref[...] = ((hf_ref[...] + hs) * gy_ref[...]).astype(o_ref.dtype)
    else:
        o_ref[...] = hs


def _rnn_direction(d, x_in, conv_w, conv_b, wa, wx, ba, bx, lam, batch, seq, hf=None, gy=None, tt_rows=512, cb=512):
    m, d_rnn = x_in.shape
    sub_w = wa.shape[2]
    reverse = d == 1
    halo = RNN_HALO
    tt_rows = _tile(seq, tt_rows)
    cb = max(_tile(d_rnn, cb), sub_w)
    n_sub = cb // sub_w
    n_tt = seq // tt_rows
    hb = tt_rows // halo
    n_hb = seq // halo

    def te(t):
        return (n_tt - 1 - t) if reverse else t

    main = pl.BlockSpec((tt_rows, cb), lambda b, c, t: (b * n_tt + te(t), c))
    vec = lambda rows: pl.BlockSpec((rows, cb), lambda b, c, t: (0, c))
    wspec = pl.BlockSpec((n_sub, sub_w, sub_w), lambda b, c, t: (c, 0, 0))
    gate_specs = [wspec, wspec, vec(1), vec(1), vec(1)]
    gate_args = [wa[d].astype(BF16), wx[d].astype(BF16), ba[d].reshape(1, -1).astype(F32),
                 bx[d].reshape(1, -1).astype(F32), lam[d].reshape(1, -1).astype(F32)]
    if reverse:
        in_specs = [main] + gate_specs + [main, main]
        args = [x_in] + gate_args + [hf, gy]
        out_specs = main
        out_shape = jax.ShapeDtypeStruct((m, d_rnn), BF16)
    else:
        prev = pl.BlockSpec((halo, cb), lambda b, c, t: (b * n_hb + jnp.maximum(t * hb - 1, 0), c))
        nxt = pl.BlockSpec((halo, cb), lambda b, c, t: (b * n_hb + jnp.minimum((t + 1) * hb, n_hb - 1), c))
        in_specs = [main, prev, nxt, vec(4), vec(1)] + gate_specs
        args = [x_in, x_in, x_in, conv_w.astype(F32), conv_b.reshape(1, -1).astype(F32)] + gate_args
        out_specs = [main, main]
        out_shape = [jax.ShapeDtypeStruct((m, d_rnn), F32), jax.ShapeDtypeStruct((m, d_rnn), F32)]
    kern = functools.partial(_rnn_kernel, reverse=reverse, n_tt=n_tt, tt_rows=tt_rows, n_sub=n_sub, sub_w=sub_w)
    return pl.pallas_call(
        kern,
        grid=(batch, d_rnn // cb, n_tt),
        in_specs=in_specs,
        out_specs=out_specs,
        out_shape=out_shape,
        scratch_shapes=[pltpu.VMEM((tt_rows // 8, 8, cb), F32), pltpu.VMEM((tt_rows // 8, 8, cb), F32),
                        pltpu.VMEM((8, cb), F32)],
        compiler_params=_cparams("arbitrary", "arbitrary", "arbitrary"),
        name="rglru_bwd" if reverse else "rglru_fwd",
    )(*args)


def _ffn_kernel(hm_ref, hp_ref, hn_ref, x1_ref, wv_ref, wg_ref, cwv_ref, cwg_ref, cbv_ref, cbg_ref, wd_ref,
                x2_ref, x2b_ref, ext_scr, u_scr, act_scr, *, n_tt, n_fc, tm, sub, nsplit):
    halo = FFN_HALO
    tt = pl.program_id(1)
    j = pl.program_id(2)
    fc = act_scr.shape[2]
    n_sub = fc // sub
    n_lt = sub // LANES
    d = x2_ref.shape[1]

    def up_piece(slot, s, k):
        def run():
            w_ref = (wv_ref, wg_ref)[k]
            u = jnp.dot(ext_scr[...], w_ref[:, s * sub:(s + 1) * sub], preferred_element_type=F32)
            for c in range(n_lt):
                u_scr[slot, k, s * n_lt + c] = u[:, c * LANES:(c + 1) * LANES]
        return run

    def conv(slot, k, c, r0, cw_ref, cb_ref):
        cols = slice(c * LANES, (c + 1) * LANES)
        out = u_scr[slot, k, c, pl.ds(halo - 1 + r0, ACT_ROWS), :] * cw_ref[0:1, cols]
        out = out + u_scr[slot, k, c, pl.ds(halo + r0, ACT_ROWS), :] * cw_ref[1:2, cols]
        out = out + u_scr[slot, k, c, pl.ds(halo + 1 + r0, ACT_ROWS), :] * cw_ref[2:3, cols]
        return out + cb_ref[:, cols]

    def act_piece(slot, c):
        def run():
            for r0 in range(0, tm, ACT_ROWS):
                val = conv(slot, 0, c, r0, cwv_ref, cbv_ref)
                gate = conv(slot, 1, c, r0, cwg_ref, cbg_ref)
                act_scr[slot, r0:r0 + ACT_ROWS, c * LANES:(c + 1) * LANES] = (jax.nn.gelu(gate) * val).astype(BF16)
        return run

    def down_piece(slot, n):
        def run():
            ncols = slice(n * nsplit, (n + 1) * nsplit)
            x2_ref[:, ncols] += jnp.dot(act_scr[slot], wd_ref[:, ncols], preferred_element_type=F32)
        return run

    def up(slot):
        return [up_piece(slot, s, k) for s in range(n_sub) for k in range(2)]

    def activate(slot):
        return [act_piece(slot, c) for c in range(fc // LANES)]

    def down(slot):
        return [down_piece(slot, n) for n in range(d // nsplit)]

    def run_interleaved(mxu_pieces, vpu_pieces):
        n_m, n_v = len(mxu_pieces), len(vpu_pieces)
        done_v = 0
        for i, piece in enumerate(mxu_pieces):
            piece()
            upto = (i + 1) * n_v // n_m if n_m else n_v
            for v in vpu_pieces[done_v:upto]:
                v()
            done_v = max(done_v, upto)
        for v in vpu_pieces[done_v:]:
            v()

    @pl.when(j == 0)
    def _():
        ext_scr[0:halo, :] = jnp.where(tt > 0, hp_ref[...], jnp.zeros_like(hp_ref))
        ext_scr[halo:halo + tm, :] = hm_ref[...]
        ext_scr[halo + tm:, :] = jnp.where(tt < n_tt - 1, hn_ref[...], jnp.zeros_like(hn_ref))
        x2_ref[...] = x1_ref[...]
        run_interleaved(up(0), [])

    @pl.when(j == 1)
    def _():
        run_interleaved(up(1), activate(0))

    for par in range(2):
        @pl.when((j >= 2) & (j < n_fc) & (lax.rem(j, 2) == par))
        def _(par=par):
            run_interleaved(up(par), activate(1 - par))
            run_interleaved(down(par), [])

    @pl.when(j == n_fc)
    def _():
        run_interleaved(down(n_fc % 2), activate((n_fc - 1) % 2))

    @pl.when(j == n_fc + 1)
    def _():
        run_interleaved(down((n_fc - 1) % 2), [])
        x2b_ref[...] = x2_ref[...].astype(BF16)


def _ffn(hn, x1, w_up, conv_w, conv_b, w_down, batch, seq, tm=512, fc=512):
    m, d = hn.shape
    d_ff = w_down.shape[0]
    halo = FFN_HALO
    tm = _tile(seq, tm)
    fc = _tile(d_ff, fc)
    sub = _tile(fc, MXU_WIDTH)
    nsplit = _tile(d, 512)
    n_tt = seq // tm
    n_fc = d_ff // fc
    hb = tm // halo
    n_hb = seq // halo
    assert n_fc >= 2
    main = pl.BlockSpec((tm, d), lambda b, t, j: (b * n_tt + t, 0), pipeline_mode=pl.Buffered(1))
    prev = pl.BlockSpec((halo, d), lambda b, t, j: (b * n_hb + jnp.maximum(t * hb - 1, 0), 0))
    nxt = pl.BlockSpec((halo, d), lambda b, t, j: (b * n_hb + jnp.minimum((t + 1) * hb, n_hb - 1), 0))
    chunk = lambda j, lag: jnp.clip(j - lag, 0, n_fc - 1)
    wv = pl.BlockSpec((None, d, fc), lambda b, t, j: (chunk(j, 0), 0, 0))
    wg = pl.BlockSpec((None, d, fc), lambda b, t, j: (n_fc + chunk(j, 0), 0, 0))
    w_up = _chunk_major(w_up, fc)
    cv = lambda rows: pl.BlockSpec((rows, fc), lambda b, t, j: (0, chunk(j, 1)))
    cg = lambda rows: pl.BlockSpec((rows, fc), lambda b, t, j: (0, n_fc + chunk(j, 1)))
    wd = pl.BlockSpec((fc, d), lambda b, t, j: (chunk(j, 2), 0))
    cb2 = conv_b.reshape(1, -1).astype(F32)
    cw = conv_w.astype(F32)
    kern = functools.partial(_ffn_kernel, n_tt=n_tt, n_fc=n_fc, tm=tm, sub=sub, nsplit=nsplit)
    return pl.pallas_call(
        kern,
        grid=(batch, n_tt, n_fc + 2),
        in_specs=[main, prev, nxt, main, wv, wg, cv(3), cg(3), cv(1), cg(1), wd],
        out_specs=[main, main],
        out_shape=[jax.ShapeDtypeStruct((m, d), F32), jax.ShapeDtypeStruct((m, d), BF16)],
        scratch_shapes=[pltpu.VMEM((tm + 2 * halo, d), BF16),
                        pltpu.VMEM((2, 2, fc // LANES, tm + 2 * halo, LANES), F32),
                        pltpu.VMEM((2, tm, fc), BF16)],
        compiler_params=_cparams("parallel", "arbitrary", "arbitrary"),
        name="conv_mlp",
    )(hn, hn, hn, x1, w_up, w_up, cw, cw, cb2, cb2, w_down)


def _ple_kernel(xb_ref, x2_ref, p_ref, wp_ref, gp_ref, wg_ref, gf_ref, o_ref, e_scr, *, n_j, tn):
    j = pl.program_id(1)
    d = o_ref.shape[1]

    @pl.when(j == 0)
    def _():
        pb = p_ref[...].astype(BF16)
        ssq = jnp.zeros((pb.shape[0], 1), F32)
        for c in range(n_j):
            pe = jnp.dot(pb, wp_ref[:, c * tn:(c + 1) * tn], preferred_element_type=F32)
            e_scr[:, c * tn:(c + 1) * tn] = pe
            ssq = ssq + jnp.sum(pe * pe, axis=-1, keepdims=True)
        e_scr[...] = e_scr[...] * lax.rsqrt(ssq * (1.0 / d) + EPS) * gp_ref[...]

    gate = _sigmoid(jnp.dot(xb_ref[...], wg_ref[...], preferred_element_type=F32))
    for jj in range(n_j):
        @pl.when(j == jj)
        def _(jj=jj):
            o_ref[:, jj * tn:(jj + 1) * tn] = x2_ref[...] + gate * e_scr[:, jj * tn:(jj + 1) * tn]

    @pl.when(j == n_j - 1)
    def _():
        o_ref[...] = _rms_scale(o_ref[...], gf_ref[...])


def _ple_final(x2b, x2, p, w_ple, g_ple, w_gate, g_final, tm=512, tn=512):
    m, d = x2.shape
    kp = p.shape[1]
    tm = _tile(m, tm)
    tn = _tile(d, tn)
    n_j = d // tn
    row = lambda w: pl.BlockSpec((tm, w), lambda i, j: (i, 0))
    full = lambda r: pl.BlockSpec((r, d), lambda i, j: (0, 0))
    return pl.pallas_call(
        functools.partial(_ple_kernel, n_j=n_j, tn=tn),
        grid=(m // tm, n_j),
        in_specs=[row(d), pl.BlockSpec((tm, tn), lambda i, j: (i, j)), row(kp), full(kp), full(1),
                  pl.BlockSpec((None, d, tn), lambda i, j: (j, 0, 0)), full(1)],
        out_specs=row(d),
        out_shape=jax.ShapeDtypeStruct((m, d), F32),
        scratch_shapes=[pltpu.VMEM((tm, d), F32)],
        compiler_params=_cparams("parallel", "arbitrary"),
        name="ple_final",
    )(x2b, x2, p, w_ple, g_ple.reshape(1, d).astype(F32), _chunk_major(w_gate, tn),
      g_final.reshape(1, d).astype(F32))


def _encoder(x3, p4, wts):
    batch, seq, d = x3.shape
    m = batch * seq
    x = x3.reshape(m, d)
    n_q = wts["attn_sink"].shape[-1]
    attn_w = wts["w_o_attn"].shape[-2]
    d_rnn = wts["w_o_rnn"].shape[-2]
    hd = attn_w // n_q
    kv_w = (wts["w_in"].shape[-1] - attn_w - 2 * d_rnn) // 2
    n_kv = kv_w // hd
    qkv_w = attn_w + 2 * kv_w
    depth = wts["w_in"].shape[0]
    assert depth == 1, "the final norm is fused into the (single) layer's embedding kernel"
    ident = lambda acc: acc

    for l in range(depth):
        w_in = wts["w_in"][l]
        h = _rmsnorm(x, wts["g_mix"][l], BF16)
        qkv = _mm(h, w_in[:, :qkv_w].astype(BF16), ident, BF16, "in_qkv")
        xr = _mm(h, w_in[:, qkv_w:qkv_w + d_rnn].astype(BF16), ident, F32, "in_xr")
        gy = _mm(h, w_in[:, qkv_w + d_rnn:].astype(BF16), jax.nn.gelu, F32, "in_gelu_y")
        gates = _mm(h, wts["w_merge_gate"][l].astype(BF16), _sigmoid, F32, "merge_gates")

        attn = _attention(qkv, wts["attn_sink"][l], batch, seq, n_q, n_kv, hd)

        lru = (wts["rnn_conv_w"][l], wts["rnn_conv_b"][l], wts["lru_wa"][l], wts["lru_wx"][l],
               wts["lru_ba"][l], wts["lru_bx"][l], wts["lru_lambda"][l], batch, seq)
        h_fwd, xc = _rnn_direction(0, xr, *lru)
        rec = _rnn_direction(1, xc, *lru, hf=h_fwd, gy=gy)

        merged = _mm_merge(attn, rec, wts["w_o_attn"][l].astype(BF16), wts["w_o_rnn"][l].astype(BF16), gates, d)
        x1, hn = _out_norm(merged, wts["w_out"][l].astype(BF16), x, wts["g_ffn"][l])
        x2, x2b = _ffn(hn, x1, wts["w_up"][l].astype(BF16), wts["ffn_conv_w"][l], wts["ffn_conv_b"][l],
                       wts["w_down"][l].astype(BF16), batch, seq)
        y = _ple_final(x2b, x2, p4[l].reshape(m, -1), wts["w_ple"][l].astype(BF16), wts["g_ple"][l],
                       wts["w_ple_gate"][l].astype(BF16), wts["g_final"])
    return y.reshape(batch, seq, d)


def kernel(x_prompt, x_sample, p_prompt, p_sample, g_mix, w_in, rnn_conv_w, rnn_conv_b, lru_wa, lru_ba, lru_wx, lru_bx, lru_lambda, attn_sink, w_o_attn, w_o_rnn, w_merge_gate, w_out, g_ffn, w_up, ffn_conv_w, ffn_conv_b, w_down, w_ple_gate, w_ple, g_ple, g_final):
    wts = dict(g_mix=g_mix, w_in=w_in, rnn_conv_w=rnn_conv_w, rnn_conv_b=rnn_conv_b, lru_wa=lru_wa, lru_ba=lru_ba,
               lru_wx=lru_wx, lru_bx=lru_bx, lru_lambda=lru_lambda, attn_sink=attn_sink, w_o_attn=w_o_attn,
               w_o_rnn=w_o_rnn, w_merge_gate=w_merge_gate, w_out=w_out, g_ffn=g_ffn, w_up=w_up,
               ffn_conv_w=ffn_conv_w, ffn_conv_b=ffn_conv_b, w_down=w_down, w_ple_gate=w_ple_gate, w_ple=w_ple,
               g_ple=g_ple, g_final=g_final)
    return (_encoder(x_prompt, p_prompt, wts), _encoder(x_sample, p_sample, wts))
```

```python
import functools

import numpy as np
import jax
import jax.numpy as jnp
from jax import lax
from jax.experimental import pallas as pl
from jax.experimental.pallas import tpu as pltpu

F32 = jnp.float32
BF16 = jnp.bfloat16

EPS = 1e-6
NEG = -1e30
ATTN_BLOCK = 128
LRU_C = 8.0
RNN_HALO = 8
FFN_HALO = 16
LANES = 128
ACT_ROWS = 64
MXU_WIDTH = 256
V7X_VMEM_LIMIT_BYTES = 60 * 1024 * 1024


def _cparams(*sem):
    return pltpu.CompilerParams(dimension_semantics=sem, vmem_limit_bytes=V7X_VMEM_LIMIT_BYTES)


def _tile(dim, pref):
    t = min(dim, pref)
    while dim % t:
        t //= 2
    return t


def _run_interleaved(mxu_pieces, vpu_pieces):
    n_m, n_v = len(mxu_pieces), len(vpu_pieces)
    done_v = 0
    for i, piece in enumerate(mxu_pieces):
        piece()
        upto = (i + 1) * n_v // n_m
        for v in vpu_pieces[done_v:upto]:
            v()
        done_v = max(done_v, upto)
    for v in vpu_pieces[done_v:]:
        v()


def _sigmoid(x):
    return 0.5 * jnp.tanh(0.5 * x) + 0.5


def _rms_scale(x, g):
    ms = jnp.mean(x * x, axis=-1, keepdims=True)
    return x * lax.rsqrt(ms + EPS) * g


def _rmsnorm_kernel(x_ref, g_ref, o_ref):
    o_ref[...] = _rms_scale(x_ref[...].astype(F32), g_ref[...]).astype(o_ref.dtype)


def _rmsnorm(x, g, out_dtype, tm=256):
    m, d = x.shape
    tm = _tile(m, tm)
    return pl.pallas_call(
        _rmsnorm_kernel,
        grid=(m // tm,),
        in_specs=[pl.BlockSpec((tm, d), lambda i: (i, 0)), pl.BlockSpec((1, d), lambda i: (0, 0))],
        out_specs=pl.BlockSpec((tm, d), lambda i: (i, 0)),
        out_shape=jax.ShapeDtypeStruct((m, d), out_dtype),
        compiler_params=_cparams("parallel"),
        name="rmsnorm",
    )(x, g.reshape(1, d).astype(F32))


def _mm_kernel(*refs, epilogue):
    a_ref, b_ref, o_ref = refs[0], refs[1], refs[-1]
    acc = jnp.dot(a_ref[...], b_ref[...], preferred_element_type=F32)
    o_ref[...] = epilogue(acc, *[r[...] for r in refs[2:-1]]).astype(o_ref.dtype)


def _mm(a, b, epilogue, out_dtype, name, extra=None, tm=1024, tn=1024):
    m, k = a.shape
    n = b.shape[1]
    tm = _tile(m, tm)
    tn = _tile(n, tn)
    tile = pl.BlockSpec((tm, tn), lambda i, j: (i, j))
    extras = [] if extra is None else [extra]
    return pl.pallas_call(
        functools.partial(_mm_kernel, epilogue=epilogue),
        grid=(m // tm, n // tn),
        in_specs=[pl.BlockSpec((tm, k), lambda i, j: (i, 0)), pl.BlockSpec((k, tn), lambda i, j: (0, j))]
        + [tile] * len(extras),
        out_specs=tile,
        out_shape=jax.ShapeDtypeStruct((m, n), out_dtype),
        compiler_params=_cparams("parallel", "arbitrary"),
        name=name,
    )(a, b, *extras)


def _merge_kernel(a_ref, r_ref, wa_ref, wr_ref, ga_ref, gr_ref, o_ref):
    pa = jnp.dot(a_ref[...], wa_ref[...], preferred_element_type=F32)
    pr = jnp.dot(r_ref[...], wr_ref[...], preferred_element_type=F32)
    o_ref[...] = (ga_ref[...] * pa + gr_ref[...] * pr).astype(o_ref.dtype)


def _mm_merge(attn, rec, w_oa, w_or, gates, d, tm=1024, tn=512):
    m = attn.shape[0]
    tm = _tile(m, tm)
    tn = _tile(d, tn)
    n_j = d // tn
    return pl.pallas_call(
        _merge_kernel,
        grid=(m // tm, n_j),
        in_specs=[pl.BlockSpec((tm, attn.shape[1]), lambda i, j: (i, 0)),
                  pl.BlockSpec((tm, rec.shape[1]), lambda i, j: (i, 0)),
                  pl.BlockSpec((w_oa.shape[0], tn), lambda i, j: (0, j)),
                  pl.BlockSpec((w_or.shape[0], tn), lambda i, j: (0, j)),
                  pl.BlockSpec((tm, tn), lambda i, j: (i, j)),
                  pl.BlockSpec((tm, tn), lambda i, j: (i, n_j + j))],
        out_specs=pl.BlockSpec((tm, tn), lambda i, j: (i, j)),
        out_shape=jax.ShapeDtypeStruct((m, d), BF16),
        compiler_params=_cparams("parallel", "arbitrary"),
        name="merge",
    )(attn, rec, w_oa, w_or, gates, gates)


def _attn_kernel(sink_ref, q_ref, kp_ref, kc_ref, kn_ref, vp_ref, vc_ref, vn_ref, o_ref, bias_scr,
                 *, nb, n_kv, group, hd, slopes):
    blk = ATTN_BLOCK
    n = pl.program_id(1)

    @pl.when((pl.program_id(0) == 0) & (n == 0))
    def _():
        t = lax.broadcasted_iota(jnp.int32, (blk, 3 * blk), 0)
        s = lax.broadcasted_iota(jnp.int32, (blk, 3 * blk), 1)
        dist = jnp.abs(s - blk - t)
        distf = dist.astype(F32)
        for hq in range(n_kv * group):
            bias_scr[hq * blk:(hq + 1) * blk, :] = jnp.where(dist <= blk, -slopes[hq] * distf, NEG)

    col = lax.broadcasted_iota(jnp.int32, (1, 3 * blk), 1)
    edge = jnp.where(((col < blk) & (n == 0)) | ((col >= 2 * blk) & (n == nb - 1)), NEG, 0.0).astype(F32)
    scale = hd ** -0.5
    n_q = n_kv * group

    q_rows = [q_ref[:, hq * hd:(hq + 1) * hd] for hq in range(n_q)]
    scores = []
    for h in range(n_kv):
        q_h = jnp.concatenate(q_rows[h * group:(h + 1) * group], axis=0)
        k_h = jnp.concatenate([r[:, h * hd:(h + 1) * hd] for r in (kp_ref, kc_ref, kn_ref)], axis=0)
        scores.append(lax.dot_general(q_h, k_h, (((1,), (1,)), ((), ())), preferred_element_type=F32))
    s = jnp.concatenate(scores, axis=0) * scale + bias_scr[...] + edge
    sk = jnp.concatenate([jnp.full((blk, 1), sink_ref[hq], F32) for hq in range(n_q)], axis=0)
    m = jnp.maximum(jnp.max(s, axis=-1, keepdims=True), sk)
    e = jnp.exp(s - m)
    inv = 1.0 / (jnp.sum(e, axis=-1, keepdims=True) + jnp.exp(sk - m))
    eb = e.astype(BF16)
    rows = group * blk
    for h in range(n_kv):
        v_h = jnp.concatenate([r[:, h * hd:(h + 1) * hd] for r in (vp_ref, vc_ref, vn_ref)], axis=0)
        o = jnp.dot(eb[h * rows:(h + 1) * rows], v_h, preferred_element_type=F32) * inv[h * rows:(h + 1) * rows]
        for g in range(group):
            o_ref[:, (h * group + g) * hd:(h * group + g + 1) * hd] = o[g * blk:(g + 1) * blk].astype(o_ref.dtype)


def _attention(qkv, sink, batch, seq, n_q, n_kv, hd):
    blk = ATTN_BLOCK
    nb = seq // blk
    attn_w, kv_w = n_q * hd, n_kv * hd
    assert attn_w % kv_w == 0 and seq % blk == 0
    kcol = attn_w // kv_w
    slopes = tuple(float(v) for v in (2.0 ** (-8.0 * (np.arange(n_q) + 1) / n_q)).astype(np.float32))

    def spec(w, colblk, shift):
        return pl.BlockSpec((blk, w), lambda b, n: (b * nb + jnp.clip(n + shift, 0, nb - 1), colblk))

    kern = functools.partial(_attn_kernel, nb=nb, n_kv=n_kv, group=n_q // n_kv, hd=hd, slopes=slopes)
    return pl.pallas_call(
        kern,
        grid=(batch, nb),
        in_specs=[pl.BlockSpec(memory_space=pltpu.SMEM),
                  spec(attn_w, 0, 0),
                  spec(kv_w, kcol, -1), spec(kv_w, kcol, 0), spec(kv_w, kcol, 1),
                  spec(kv_w, kcol + 1, -1), spec(kv_w, kcol + 1, 0), spec(kv_w, kcol + 1, 1)],
        out_specs=pl.BlockSpec((blk, attn_w), lambda b, n: (b * nb + n, 0)),
        out_shape=jax.ShapeDtypeStruct((batch * seq, attn_w), BF16),
        scratch_shapes=[pltpu.VMEM((n_q * blk, 3 * blk), F32)],
        compiler_params=_cparams("arbitrary", "arbitrary"),
        name="banded_gqa",
    )(sink.astype(F32), qkv, qkv, qkv, qkv, qkv, qkv, qkv)


def _lru_scan(a_scr, b_scr, carry_scr, n_slab, reverse):
    row = lax.broadcasted_iota(jnp.int32, a_scr.shape[1:], 0)

    def slab(i, carry):
        idx = (n_slab - 1 - i) if reverse else i
        a = a_scr[idx]
        b = b_scr[idx]
        for k in (1, 2, 4):
            keep = (row < 8 - k) if reverse else (row >= k)
            shift = (8 - k) if reverse else k
            a_sh = jnp.where(keep, pltpu.roll(a, shift, 0), 1.0)
            b_sh = jnp.where(keep, pltpu.roll(b, shift, 0), 0.0)
            b = b + a * b_sh
            a = a * a_sh
        h = a * carry + b
        b_scr[idx] = h
        return jnp.broadcast_to(h[0:1] if reverse else h[7:8], h.shape)

    carry_scr[...] = lax.fori_loop(0, n_slab, slab, carry_scr[...], unroll=4)


def _rnn_kernel(*refs, reverse, n_tt, tt_rows, n_sub, sub_w):
    if reverse:
        (xc_ref, wa_ref, wx_ref, ba_ref, bx_ref, lam_ref, hf_ref, gy_ref, o_ref, a_scr, b_scr, carry_scr) = refs
    else:
        (xm_ref, xp_ref, xn_ref, cw_ref, cb_ref, wa_ref, wx_ref, ba_ref, bx_ref, lam_ref,
         o_ref, xc_ref, a_scr, b_scr, carry_scr) = refs
    halo = RNN_HALO
    tt = pl.program_id(2)

    @pl.when(tt == 0)
    def _():
        carry_scr[...] = jnp.zeros_like(carry_scr)

    if reverse:
        xc = xc_ref[...]
    else:
        xp = jnp.where(tt > 0, xp_ref[...], 0.0)
        xn = jnp.where(tt < n_tt - 1, xn_ref[...], 0.0)
        ext = jnp.concatenate([xp, xm_ref[...], xn], axis=0)
        n_ext = tt_rows + 2 * halo
        cw = cw_ref[...]
        xc = pltpu.roll(ext, 1, 0)[halo:halo + tt_rows] * cw[0:1]
        xc = xc + ext[halo:halo + tt_rows] * cw[1:2]
        xc = xc + pltpu.roll(ext, n_ext - 1, 0)[halo:halo + tt_rows] * cw[2:3]
        xc = xc + pltpu.roll(ext, n_ext - 2, 0)[halo:halo + tt_rows] * cw[3:4]
        xc = xc + cb_ref[...]
        xc_ref[...] = xc

    xcb = xc.astype(BF16)
    r_pre = jnp.concatenate(
        [jnp.dot(xcb[:, s * sub_w:(s + 1) * sub_w], wa_ref[s], preferred_element_type=F32) for s in range(n_sub)],
        axis=1)
    i_pre = jnp.concatenate(
        [jnp.dot(xcb[:, s * sub_w:(s + 1) * sub_w], wx_ref[s], preferred_element_type=F32) for s in range(n_sub)],
        axis=1)
    r = _sigmoid(r_pre + ba_ref[...])
    ig = _sigmoid(i_pre + bx_ref[...])
    log_a = (-LRU_C) * r * jax.nn.softplus(-lam_ref[...])
    a = jnp.exp(log_a)
    one_minus_a2 = jnp.tanh(-log_a) * (1.0 + a * a)
    a_scr[...] = a.reshape(a_scr.shape)
    b_scr[...] = (jnp.sqrt(one_minus_a2) * (ig * xc)).reshape(b_scr.shape)

    _lru_scan(a_scr, b_scr, carry_scr, tt_rows // 8, reverse)
    hs = b_scr[...].reshape(tt_rows, -1)
    if reverse:
        o_ref[...] = ((hf_ref[...] + hs) * gy_ref[...]).astype(o_ref.dtype)
    else:
        o_ref[...] = hs


def _rnn_direction(d, x_in, conv_w, conv_b, wa, wx, ba, bx, lam, batch, seq, hf=None, gy=None, tt_rows=512, cb=512):
    m, d_rnn = x_in.shape
    sub_w = wa.shape[2]
    reverse = d == 1
    halo = RNN_HALO
    tt_rows = _tile(seq, tt_rows)
    cb = max(_tile(d_rnn, cb), sub_w)
    n_sub = cb // sub_w
    n_tt = seq // tt_rows
    hb = tt_rows // halo
    n_hb = seq // halo

    def te(t):
        return (n_tt - 1 - t) if reverse else t

    main = pl.BlockSpec((tt_rows, cb), lambda b, c, t: (b * n_tt + te(t), c))
    vec = lambda rows: pl.BlockSpec((rows, cb), lambda b, c, t: (0, c))
    wspec = pl.BlockSpec((n_sub, sub_w, sub_w), lambda b, c, t: (c, 0, 0))
    gate_specs = [wspec, wspec, vec(1), vec(1), vec(1)]
    gate_args = [wa[d].astype(BF16), wx[d].astype(BF16), ba[d].reshape(1, -1).astype(F32),
                 bx[d].reshape(1, -1).astype(F32), lam[d].reshape(1, -1).astype(F32)]
    if reverse:
        in_specs = [main] + gate_specs + [main, main]
        args = [x_in] + gate_args + [hf, gy]
        out_specs = main
        out_shape = jax.ShapeDtypeStruct((m, d_rnn), BF16)
    else:
        prev = pl.BlockSpec((halo, cb), lambda b, c, t: (b * n_hb + jnp.maximum(t * hb - 1, 0), c))
        nxt = pl.BlockSpec((halo, cb), lambda b, c, t: (b * n_hb + jnp.minimum((t + 1) * hb, n_hb - 1), c))
        in_specs = [main, prev, nxt, vec(4), vec(1)] + gate_specs
        args = [x_in, x_in, x_in, conv_w.astype(F32), conv_b.reshape(1, -1).astype(F32)] + gate_args
        out_specs = [main, main]
        out_shape = [jax.ShapeDtypeStruct((m, d_rnn), F32), jax.ShapeDtypeStruct((m, d_rnn), F32)]
    kern = functools.partial(_rnn_kernel, reverse=reverse, n_tt=n_tt, tt_rows=tt_rows, n_sub=n_sub, sub_w=sub_w)
    return pl.pallas_call(
        kern,
        grid=(batch, d_rnn // cb, n_tt),
        in_specs=in_specs,
        out_specs=out_specs,
        out_shape=out_shape,
        scratch_shapes=[pltpu.VMEM((tt_rows // 8, 8, cb), F32), pltpu.VMEM((tt_rows // 8, 8, cb), F32),
                        pltpu.VMEM((8, cb), F32)],
        compiler_params=_cparams("arbitrary", "arbitrary", "arbitrary"),
        name="rglru_bwd" if reverse else "rglru_fwd",
    )(*args)


def _ffn_kernel(x1_ref, xp_ref, xn_ref, g_ref, wv_ref, wg_ref, cwv_ref, cwg_ref, cbv_ref, cbg_ref, wd_ref,
                x2_ref, x2b_ref, ext_scr, u_scr, act_scr, *, n_tt, n_fc, tm, sub, nsplit):
    halo = FFN_HALO
    tt = pl.program_id(1)
    j = pl.program_id(2)
    fc = act_scr.shape[2]
    n_sub = fc // sub
    n_lt = sub // LANES
    d = x2_ref.shape[1]

    def up_piece(slot, s, k):
        def run():
            w_ref = (wv_ref, wg_ref)[k]
            u = jnp.dot(ext_scr[...], w_ref[:, s * sub:(s + 1) * sub], preferred_element_type=F32)
            for c in range(n_lt):
                u_scr[slot, k, s * n_lt + c] = u[:, c * LANES:(c + 1) * LANES]
        return run

    def conv(slot, k, c, r0, cw_ref, cb_ref):
        cols = slice(c * LANES, (c + 1) * LANES)
        out = u_scr[slot, k, c, pl.ds(halo - 1 + r0, ACT_ROWS), :] * cw_ref[0:1, cols]
        out = out + u_scr[slot, k, c, pl.ds(halo + r0, ACT_ROWS), :] * cw_ref[1:2, cols]
        out = out + u_scr[slot, k, c, pl.ds(halo + 1 + r0, ACT_ROWS), :] * cw_ref[2:3, cols]
        return out + cb_ref[:, cols]

    def act_piece(slot, c):
        def run():
            for r0 in range(0, tm, ACT_ROWS):
                val = conv(slot, 0, c, r0, cwv_ref, cbv_ref)
                gate = conv(slot, 1, c, r0, cwg_ref, cbg_ref)
                act_scr[slot, r0:r0 + ACT_ROWS, c * LANES:(c + 1) * LANES] = (jax.nn.gelu(gate) * val).astype(BF16)
        return run

    def down_piece(slot, n):
        def run():
            ncols = slice(n * nsplit, (n + 1) * nsplit)
            x2_ref[:, ncols] += jnp.dot(act_scr[slot], wd_ref[:, ncols], preferred_element_type=F32)
        return run

    def up(slot):
        return [up_piece(slot, s, k) for s in range(n_sub) for k in range(2)]

    def activate(slot):
        return [act_piece(slot, c) for c in range(fc // LANES)]

    def down(slot):
        return [down_piece(slot, n) for n in range(d // nsplit)]

    @pl.when(j == 0)
    def _():
        g = g_ref[...]
        zero = jnp.zeros((halo, d), BF16)
        ext_scr[0:halo, :] = jnp.where(tt > 0, _rms_scale(xp_ref[...], g).astype(BF16), zero)
        for r0 in range(0, tm, ACT_ROWS):
            rows = x1_ref[r0:r0 + ACT_ROWS, :]
            x2_ref[r0:r0 + ACT_ROWS, :] = rows
            ext_scr[halo + r0:halo + r0 + ACT_ROWS, :] = _rms_scale(rows, g).astype(BF16)
        ext_scr[halo + tm:, :] = jnp.where(tt < n_tt - 1, _rms_scale(xn_ref[...], g).astype(BF16), zero)
        _run_interleaved(up(0), [])

    @pl.when(j == 1)
    def _():
        _run_interleaved(up(1), activate(0))

    for par in range(2):
        @pl.when((j >= 2) & (j < n_fc) & (lax.rem(j, 2) == par))
        def _(par=par):
            _run_interleaved(up(par), activate(1 - par))
            _run_interleaved(down(par), [])

    @pl.when(j == n_fc)
    def _():
        _run_interleaved(down(n_fc % 2), activate((n_fc - 1) % 2))

    @pl.when(j == n_fc + 1)
    def _():
        _run_interleaved(down((n_fc - 1) % 2), [])
        x2b_ref[...] = x2_ref[...].astype(BF16)


def _ffn(x1, g, w_up, conv_w, conv_b, w_down, batch, seq, tm=512, fc=512):
    m, d = x1.shape
    d_ff = w_down.shape[0]
    halo = FFN_HALO
    tm = _tile(seq, tm)
    fc = _tile(d_ff, fc)
    sub = _tile(fc, MXU_WIDTH)
    nsplit = _tile(d, 512)
    n_tt = seq // tm
    n_fc = d_ff // fc
    hb = tm // halo
    n_hb = seq // halo
    assert n_fc >= 2 and tm % ACT_ROWS == 0
    main = pl.BlockSpec((tm, d), lambda b, t, j: (b * n_tt + t, 0), pipeline_mode=pl.Buffered(1))
    prev = pl.BlockSpec((halo, d), lambda b, t, j: (b * n_hb + jnp.maximum(t * hb - 1, 0), 0))
    nxt = pl.BlockSpec((halo, d), lambda b, t, j: (b * n_hb + jnp.minimum((t + 1) * hb, n_hb - 1), 0))
    chunk = lambda j, lag: jnp.clip(j - lag, 0, n_fc - 1)
    wv = pl.BlockSpec((d, fc), lambda b, t, j: (0, chunk(j, 0)))
    wg = pl.BlockSpec((d, fc), lambda b, t, j: (0, n_fc + chunk(j, 0)))
    cv = lambda rows: pl.BlockSpec((rows, fc), lambda b, t, j: (0, chunk(j, 1)))
    cg = lambda rows: pl.BlockSpec((rows, fc), lambda b, t, j: (0, n_fc + chunk(j, 1)))
    wd = pl.BlockSpec((fc, d), lambda b, t, j: (chunk(j, 2), 0))
    cb2 = conv_b.reshape(1, -1).astype(F32)
    cw = conv_w.astype(F32)
    kern = functools.partial(_ffn_kernel, n_tt=n_tt, n_fc=n_fc, tm=tm, sub=sub, nsplit=nsplit)
    return pl.pallas_call(
        kern,
        grid=(batch, n_tt, n_fc + 2),
        in_specs=[main, prev, nxt, pl.BlockSpec((1, d), lambda b, t, j: (0, 0)),
                  wv, wg, cv(3), cg(3), cv(1), cg(1), wd],
        out_specs=[main, main],
        out_shape=[jax.ShapeDtypeStruct((m, d), F32), jax.ShapeDtypeStruct((m, d), BF16)],
        scratch_shapes=[pltpu.VMEM((tm + 2 * halo, d), BF16),
                        pltpu.VMEM((2, 2, fc // LANES, tm + 2 * halo, LANES), F32),
                        pltpu.VMEM((2, tm, fc), BF16)],
        compiler_params=_cparams("parallel", "arbitrary", "arbitrary"),
        name="norm_conv_mlp",
    )(x1, x1, x1, g.reshape(1, d).astype(F32), w_up, w_up, cw, cw, cb2, cb2, w_down)


def _ple_kernel(xb_ref, x2_ref, p_ref, wp_ref, gp_ref, wg_ref, gf_ref, o_ref, e_scr, *, n_j, tn):
    j = pl.program_id(1)
    d = o_ref.shape[1]

    @pl.when(j == 0)
    def _():
        pb = p_ref[...].astype(BF16)
        ssq = jnp.zeros((pb.shape[0], 1), F32)
        for c in range(n_j):
            pe = jnp.dot(pb, wp_ref[:, c * tn:(c + 1) * tn], preferred_element_type=F32)
            e_scr[:, c * tn:(c + 1) * tn] = pe
            ssq = ssq + jnp.sum(pe * pe, axis=-1, keepdims=True)
        e_scr[...] = e_scr[...] * lax.rsqrt(ssq * (1.0 / d) + EPS) * gp_ref[...]

    gate = _sigmoid(jnp.dot(xb_ref[...], wg_ref[...], preferred_element_type=F32))
    for jj in range(n_j):
        @pl.when(j == jj)
        def _(jj=jj):
            o_ref[:, jj * tn:(jj + 1) * tn] = x2_ref[...] + gate * e_scr[:, jj * tn:(jj + 1) * tn]

    @pl.when(j == n_j - 1)
    def _():
        o_ref[...] = _rms_scale(o_ref[...], gf_ref[...])


def _ple_final(x2b, x2, p, w_ple, g_ple, w_gate, g_final, tm=512, tn=512):
    m, d = x2.shape
    kp = p.shape[1]
    tm = _tile(m, tm)
    tn = _tile(d, tn)
    n_j = d // tn
    row = lambda w: pl.BlockSpec((tm, w), lambda i, j: (i, 0))
    full = lambda r: pl.BlockSpec((r, d), lambda i, j: (0, 0))
    return pl.pallas_call(
        functools.partial(_ple_kernel, n_j=n_j, tn=tn),
        grid=(m // tm, n_j),
        in_specs=[row(d), pl.BlockSpec((tm, tn), lambda i, j: (i, j)), row(kp), full(kp), full(1),
                  pl.BlockSpec((d, tn), lambda i, j: (0, j)), full(1)],
        out_specs=row(d),
        out_shape=jax.ShapeDtypeStruct((m, d), F32),
        scratch_shapes=[pltpu.VMEM((tm, d), F32)],
        compiler_params=_cparams("parallel", "arbitrary"),
        name="ple_final",
    )(x2b, x2, p, w_ple, g_ple.reshape(1, d).astype(F32), w_gate, g_final.reshape(1, d).astype(F32))


def _encoder(x3, p4, wts):
    batch, seq, d = x3.shape
    m = batch * seq
    x = x3.reshape(m, d)
    n_q = wts["attn_sink"].shape[-1]
    attn_w = wts["w_o_attn"].shape[-2]
    d_rnn = wts["w_o_rnn"].shape[-2]
    hd = attn_w // n_q
    kv_w = (wts["w_in"].shape[-1] - attn_w - 2 * d_rnn) // 2
    n_kv = kv_w // hd
    qkv_w = attn_w + 2 * kv_w
    depth = wts["w_in"].shape[0]
    assert depth == 1, "the final norm is fused into the (single) layer's embedding kernel"
    ident = lambda acc: acc

    for l in range(depth):
        w_in = wts["w_in"][l]
        h = _rmsnorm(x, wts["g_mix"][l], BF16)
        qkv = _mm(h, w_in[:, :qkv_w].astype(BF16), ident, BF16, "in_qkv")
        xr = _mm(h, w_in[:, qkv_w:qkv_w + d_rnn].astype(BF16), ident, F32, "in_xr")
        gy = _mm(h, w_in[:, qkv_w + d_rnn:].astype(BF16), jax.nn.gelu, F32, "in_gelu_y")
        gates = _mm(h, wts["w_merge_gate"][l].astype(BF16), _sigmoid, F32, "merge_gates")

        attn = _attention(qkv, wts["attn_sink"][l], batch, seq, n_q, n_kv, hd)

        lru = (wts["rnn_conv_w"][l], wts["rnn_conv_b"][l], wts["lru_wa"][l], wts["lru_wx"][l],
               wts["lru_ba"][l], wts["lru_bx"][l], wts["lru_lambda"][l], batch, seq)
        h_fwd, xc = _rnn_direction(0, xr, *lru)
        rec = _rnn_direction(1, xc, *lru, hf=h_fwd, gy=gy)

        merged = _mm_merge(attn, rec, wts["w_o_attn"][l].astype(BF16), wts["w_o_rnn"][l].astype(BF16), gates, d)
        x1 = _mm(merged, wts["w_out"][l].astype(BF16), lambda acc, res: res + acc, F32, "out_residual", extra=x,
                 tn=512)
        x2, x2b = _ffn(x1, wts["g_ffn"][l], wts["w_up"][l].astype(BF16), wts["ffn_conv_w"][l],
                       wts["ffn_conv_b"][l], wts["w_down"][l].astype(BF16), batch, seq)
        y = _ple_final(x2b, x2, p4[l].reshape(m, -1), wts["w_ple"][l].astype(BF16), wts["g_ple"][l],
                       wts["w_ple_gate"][l].astype(BF16), wts["g_final"])
    return y.reshape(batch, seq, d)


def kernel(x_prompt, x_sample, p_prompt, p_sample, g_mix, w_in, rnn_conv_w, rnn_conv_b, lru_wa, lru_ba, lru_wx, lru_bx, lru_lambda, attn_sink, w_o_attn, w_o_rnn, w_merge_gate, w_out, g_ffn, w_up, ffn_conv_w, ffn_conv_b, w_down, w_ple_gate, w_ple, g_ple, g_final):
    wts = dict(g_mix=g_mix, w_in=w_in, rnn_conv_w=rnn_conv_w, rnn_conv_b=rnn_conv_b, lru_wa=lru_wa, lru_ba=lru_ba,
               lru_wx=lru_wx, lru_bx=lru_bx, lru_lambda=lru_lambda, attn_sink=attn_sink, w_o_attn=w_o_attn,
               w_o_rnn=w_o_rnn, w_merge_gate=w_merge_gate, w_out=w_out, g_ffn=g_ffn, w_up=w_up,
               ffn_conv_w=ffn_conv_w, ffn_conv_b=ffn_conv_b, w_down=w_down, w_ple_gate=w_ple_gate, w_ple=w_ple,
               g_ple=g_ple, g_final=g_final)
    return (_encoder(x_prompt, p_prompt, wts), _encoder(x_sample, p_sample, wts))
```

```python
import functools

import numpy as np
import jax
import jax.numpy as jnp
from jax import lax
from jax.experimental import pallas as pl
from jax.experimental.pallas import tpu as pltpu

F32 = jnp.float32
BF16 = jnp.bfloat16

EPS = 1e-6
NEG = -1e30
ATTN_BLOCK = 128
LRU_C = 8.0
RNN_HALO = 8
FFN_HALO = 16
LANES = 128
ACT_ROWS = 64
MXU_WIDTH = 256
V7X_VMEM_LIMIT_BYTES = 60 * 1024 * 1024


def _cparams(*sem):
    return pltpu.CompilerParams(dimension_semantics=sem, vmem_limit_bytes=V7X_VMEM_LIMIT_BYTES)


def _tile(dim, pref):
    t = min(dim, pref)
    while dim % t:
        t //= 2
    return t


def _run_interleaved(mxu_pieces, vpu_pieces):
    n_m, n_v = len(mxu_pieces), len(vpu_pieces)
    done_v = 0
    for i, piece in enumerate(mxu_pieces):
        piece()
        upto = (i + 1) * n_v // n_m
        for v in vpu_pieces[done_v:upto]:
            v()
        done_v = max(done_v, upto)
    for v in vpu_pieces[done_v:]:
        v()


def _sigmoid(x):
    return 0.5 * jnp.tanh(0.5 * x) + 0.5


def _rms_scale(x, g):
    ms = jnp.mean(x * x, axis=-1, keepdims=True)
    return x * lax.rsqrt(ms + EPS) * g


def _rmsnorm_kernel(x_ref, g_ref, o_ref):
    o_ref[...] = _rms_scale(x_ref[...].astype(F32), g_ref[...]).astype(o_ref.dtype)


def _rmsnorm(x, g, out_dtype, tm=256):
    m, d = x.shape
    tm = _tile(m, tm)
    return pl.pallas_call(
        _rmsnorm_kernel,
        grid=(m // tm,),
        in_specs=[pl.BlockSpec((tm, d), lambda i: (i, 0)), pl.BlockSpec((1, d), lambda i: (0, 0))],
        out_specs=pl.BlockSpec((tm, d), lambda i: (i, 0)),
        out_shape=jax.ShapeDtypeStruct((m, d), out_dtype),
        compiler_params=_cparams("parallel"),
        name="rmsnorm",
    )(x, g.reshape(1, d).astype(F32))


def _mm_kernel(*refs, epilogue):
    a_ref, b_ref, o_ref = refs[0], refs[1], refs[-1]
    acc = jnp.dot(a_ref[...], b_ref[...], preferred_element_type=F32)
    o_ref[...] = epilogue(acc, *[r[...] for r in refs[2:-1]]).astype(o_ref.dtype)


def _mm(a, b, epilogue, out_dtype, name, extra=None, tm=1024, tn=1024):
    m, k = a.shape
    n = b.shape[1]
    tm = _tile(m, tm)
    tn = _tile(n, tn)
    tile = pl.BlockSpec((tm, tn), lambda i, j: (i, j))
    extras = [] if extra is None else [extra]
    return pl.pallas_call(
        functools.partial(_mm_kernel, epilogue=epilogue),
        grid=(m // tm, n // tn),
        in_specs=[pl.BlockSpec((tm, k), lambda i, j: (i, 0)), pl.BlockSpec((k, tn), lambda i, j: (0, j))]
        + [tile] * len(extras),
        out_specs=tile,
        out_shape=jax.ShapeDtypeStruct((m, n), out_dtype),
        compiler_params=_cparams("parallel", "arbitrary"),
        name=name,
    )(a, b, *extras)


def _merge_kernel(a_ref, r_ref, wa_ref, wr_ref, ga_ref, gr_ref, o_ref):
    pa = jnp.dot(a_ref[...], wa_ref[...], preferred_element_type=F32)
    pr = jnp.dot(r_ref[...], wr_ref[...], preferred_element_type=F32)
    o_ref[...] = (ga_ref[...] * pa + gr_ref[...] * pr).astype(o_ref.dtype)


def _mm_merge(attn, rec, w_oa, w_or, gates, d, tm=1024, tn=512):
    m = attn.shape[0]
    tm = _tile(m, tm)
    tn = _tile(d, tn)
    n_j = d // tn
    return pl.pallas_call(
        _merge_kernel,
        grid=(m // tm, n_j),
        in_specs=[pl.BlockSpec((tm, attn.shape[1]), lambda i, j: (i, 0)),
                  pl.BlockSpec((tm, rec.shape[1]), lambda i, j: (i, 0)),
                  pl.BlockSpec((w_oa.shape[0], tn), lambda i, j: (0, j)),
                  pl.BlockSpec((w_or.shape[0], tn), lambda i, j: (0, j)),
                  pl.BlockSpec((tm, tn), lambda i, j: (i, j)),
                  pl.BlockSpec((tm, tn), lambda i, j: (i, n_j + j))],
        out_specs=pl.BlockSpec((tm, tn), lambda i, j: (i, j)),
        out_shape=jax.ShapeDtypeStruct((m, d), BF16),
        compiler_params=_cparams("parallel", "arbitrary"),
        name="merge",
    )(attn, rec, w_oa, w_or, gates, gates)


def _attn_kernel(sink_ref, q_ref, kp_ref, kc_ref, kn_ref, vp_ref, vc_ref, vn_ref, o_ref, bias_scr,
                 *, n_steps, n_kv, group, hd, slopes):
    blk = ATTN_BLOCK
    qr = q_ref.shape[0]
    nk = qr + 2 * blk
    n = pl.program_id(1)

    @pl.when((pl.program_id(0) == 0) & (n == 0))
    def _():
        t = lax.broadcasted_iota(jnp.int32, (qr, nk), 0)
        s = lax.broadcasted_iota(jnp.int32, (qr, nk), 1)
        dist = jnp.abs(s - blk - t)
        distf = dist.astype(F32)
        for hq in range(n_kv * group):
            bias_scr[hq * qr:(hq + 1) * qr, :] = jnp.where(dist <= blk, -slopes[hq] * distf, NEG)

    col = lax.broadcasted_iota(jnp.int32, (1, nk), 1)
    edge = jnp.where(((col < blk) & (n == 0)) | ((col >= blk + qr) & (n == n_steps - 1)), NEG, 0.0).astype(F32)
    scale = hd ** -0.5
    n_q = n_kv * group

    q_rows = [q_ref[:, hq * hd:(hq + 1) * hd] for hq in range(n_q)]
    scores = []
    for h in range(n_kv):
        q_h = jnp.concatenate(q_rows[h * group:(h + 1) * group], axis=0)
        k_h = jnp.concatenate([r[:, h * hd:(h + 1) * hd] for r in (kp_ref, kc_ref, kn_ref)], axis=0)
        scores.append(lax.dot_general(q_h, k_h, (((1,), (1,)), ((), ())), preferred_element_type=F32))
    s = jnp.concatenate(scores, axis=0) * scale + bias_scr[...] + edge
    sk = jnp.concatenate([jnp.full((qr, 1), sink_ref[hq], F32) for hq in range(n_q)], axis=0)
    m = jnp.maximum(jnp.max(s, axis=-1, keepdims=True), sk)
    e = jnp.exp(s - m)
    inv = 1.0 / (jnp.sum(e, axis=-1, keepdims=True) + jnp.exp(sk - m))
    eb = e.astype(BF16)
    rows = group * qr
    for h in range(n_kv):
        v_h = jnp.concatenate([r[:, h * hd:(h + 1) * hd] for r in (vp_ref, vc_ref, vn_ref)], axis=0)
        o = jnp.dot(eb[h * rows:(h + 1) * rows], v_h, preferred_element_type=F32) * inv[h * rows:(h + 1) * rows]
        for g in range(group):
            o_ref[:, (h * group + g) * hd:(h * group + g + 1) * hd] = o[g * qr:(g + 1) * qr].astype(o_ref.dtype)


def _attention(qkv, sink, batch, seq, n_q, n_kv, hd, q_blocks=1):
    blk = ATTN_BLOCK
    qr = q_blocks * blk
    nb = seq // blk
    n_steps = seq // qr
    attn_w, kv_w = n_q * hd, n_kv * hd
    assert attn_w % kv_w == 0 and seq % qr == 0
    kcol = attn_w // kv_w
    slopes = tuple(float(v) for v in (2.0 ** (-8.0 * (np.arange(n_q) + 1) / n_q)).astype(np.float32))

    def cur(w, colblk):
        return pl.BlockSpec((qr, w), lambda b, n: (b * n_steps + n, colblk))

    def side(colblk, first_blk):
        return pl.BlockSpec((blk, kv_w),
                            lambda b, n: (b * nb + jnp.clip(n * q_blocks + first_blk, 0, nb - 1), colblk))

    kern = functools.partial(_attn_kernel, n_steps=n_steps, n_kv=n_kv, group=n_q // n_kv, hd=hd, slopes=slopes)
    return pl.pallas_call(
        kern,
        grid=(batch, n_steps),
        in_specs=[pl.BlockSpec(memory_space=pltpu.SMEM),
                  cur(attn_w, 0),
                  side(kcol, -1), cur(kv_w, kcol), side(kcol, q_blocks),
                  side(kcol + 1, -1), cur(kv_w, kcol + 1), side(kcol + 1, q_blocks)],
        out_specs=cur(attn_w, 0),
        out_shape=jax.ShapeDtypeStruct((batch * seq, attn_w), BF16),
        scratch_shapes=[pltpu.VMEM((n_q * qr, qr + 2 * blk), F32)],
        compiler_params=_cparams("arbitrary", "arbitrary"),
        name="banded_gqa",
    )(sink.astype(F32), qkv, qkv, qkv, qkv, qkv, qkv, qkv)


def _lru_scan(a_scr, b_scr, carry_scr, n_slab, reverse):
    row = lax.broadcasted_iota(jnp.int32, a_scr.shape[1:], 0)

    def slab(i, carry):
        idx = (n_slab - 1 - i) if reverse else i
        a = a_scr[idx]
        b = b_scr[idx]
        for k in (1, 2, 4):
            keep = (row < 8 - k) if reverse else (row >= k)
            shift = (8 - k) if reverse else k
            a_sh = jnp.where(keep, pltpu.roll(a, shift, 0), 1.0)
            b_sh = jnp.where(keep, pltpu.roll(b, shift, 0), 0.0)
            b = b + a * b_sh
            a = a * a_sh
        h = a * carry + b
        b_scr[idx] = h
        return jnp.broadcast_to(h[0:1] if reverse else h[7:8], h.shape)

    carry_scr[...] = lax.fori_loop(0, n_slab, slab, carry_scr[...], unroll=4)


def _rnn_kernel(*refs, reverse, n_tt, tt_rows, n_sub, sub_w):
    if reverse:
        (xc_ref, wa_ref, wx_ref, ba_ref, bx_ref, lam_ref, hf_ref, gy_ref, o_ref, a_scr, b_scr, carry_scr) = refs
    else:
        (xm_ref, xp_ref, xn_ref, cw_ref, cb_ref, wa_ref, wx_ref, ba_ref, bx_ref, lam_ref,
         o_ref, xc_ref, a_scr, b_scr, carry_scr) = refs
    halo = RNN_HALO
    tt = pl.program_id(2)

    @pl.when(tt == 0)
    def _():
        carry_scr[...] = jnp.zeros_like(carry_scr)

    if reverse:
        xc = xc_ref[...]
    else:
        xp = jnp.where(tt > 0, xp_ref[...], 0.0)
        xn = jnp.where(tt < n_tt - 1, xn_ref[...], 0.0)
        ext = jnp.concatenate([xp, xm_ref[...], xn], axis=0)
        n_ext = tt_rows + 2 * halo
        cw = cw_ref[...]
        xc = pltpu.roll(ext, 1, 0)[halo:halo + tt_rows] * cw[0:1]
        xc = xc + ext[halo:halo + tt_rows] * cw[1:2]
        xc = xc + pltpu.roll(ext, n_ext - 1, 0)[halo:halo + tt_rows] * cw[2:3]
        xc = xc + pltpu.roll(ext, n_ext - 2, 0)[halo:halo + tt_rows] * cw[3:4]
        xc = xc + cb_ref[...]
        xc_ref[...] = xc

    xcb = xc.astype(BF16)
    r_pre = jnp.concatenate(
        [jnp.dot(xcb[:, s * sub_w:(s + 1) * sub_w], wa_ref[s], preferred_element_type=F32) for s in range(n_sub)],
        axis=1)
    i_pre = jnp.concatenate(
        [jnp.dot(xcb[:, s * sub_w:(s + 1) * sub_w], wx_ref[s], preferred_element_type=F32) for s in range(n_sub)],
        axis=1)
    half_c_sp = (-0.5 * LRU_C) * jax.nn.softplus(-lam_ref[...])
    log_a = half_c_sp * jnp.tanh(r_pre + ba_ref[...]) + half_c_sp
    ig = 0.5 * jnp.tanh(i_pre + bx_ref[...]) + 0.5
    a = jnp.exp(log_a)
    one_minus_a2 = jnp.tanh(-log_a) * (1.0 + a * a)
    a_scr[...] = a.reshape(a_scr.shape)
    b_scr[...] = (jnp.sqrt(one_minus_a2) * (ig * xc)).reshape(b_scr.shape)

    _lru_scan(a_scr, b_scr, carry_scr, tt_rows // 8, reverse)
    hs = b_scr[...].reshape(tt_rows, -1)
    if reverse:
        o_ref[...] = ((hf_ref[...] + hs) * gy_ref[...]).astype(o_ref.dtype)
    else:
        o_ref[...] = hs


def _rnn_direction(d, x_in, conv_w, conv_b, wa, wx, ba, bx, lam, batch, seq, hf=None, gy=None, tt_rows=512, cb=512):
    m, d_rnn = x_in.shape
    sub_w = wa.shape[2]
    reverse = d == 1
    halo = RNN_HALO
    tt_rows = _tile(seq, tt_rows)
    cb = max(_tile(d_rnn, cb), sub_w)
    n_sub = cb // sub_w
    n_tt = seq // tt_rows
    hb = tt_rows // halo
    n_hb = seq // halo

    def te(t):
        return (n_tt - 1 - t) if reverse else t

    main = pl.BlockSpec((tt_rows, cb), lambda b, c, t: (b * n_tt + te(t), c))
    vec = lambda rows: pl.BlockSpec((rows, cb), lambda b, c, t: (0, c))
    wspec = pl.BlockSpec((n_sub, sub_w, sub_w), lambda b, c, t: (c, 0, 0))
    gate_specs = [wspec, wspec, vec(1), vec(1), vec(1)]
    gate_args = [(0.5 * wa[d]).astype(BF16), (0.5 * wx[d]).astype(BF16),
                 (0.5 * ba[d]).reshape(1, -1).astype(F32), (0.5 * bx[d]).reshape(1, -1).astype(F32),
                 lam[d].reshape(1, -1).astype(F32)]
    if reverse:
        in_specs = [main] + gate_specs + [main, main]
        args = [x_in] + gate_args + [hf, gy]
        out_specs = main
        out_shape = jax.ShapeDtypeStruct((m, d_rnn), BF16)
    else:
        prev = pl.BlockSpec((halo, cb), lambda b, c, t: (b * n_hb + jnp.maximum(t * hb - 1, 0), c))
        nxt = pl.BlockSpec((halo, cb), lambda b, c, t: (b * n_hb + jnp.minimum((t + 1) * hb, n_hb - 1), c))
        in_specs = [main, prev, nxt, vec(4), vec(1)] + gate_specs
        args = [x_in, x_in, x_in, conv_w.astype(F32), conv_b.reshape(1, -1).astype(F32)] + gate_args
        out_specs = [main, main]
        out_shape = [jax.ShapeDtypeStruct((m, d_rnn), F32), jax.ShapeDtypeStruct((m, d_rnn), F32)]
    kern = functools.partial(_rnn_kernel, reverse=reverse, n_tt=n_tt, tt_rows=tt_rows, n_sub=n_sub, sub_w=sub_w)
    return pl.pallas_call(
        kern,
        grid=(batch, d_rnn // cb, n_tt),
        in_specs=in_specs,
        out_specs=out_specs,
        out_shape=out_shape,
        scratch_shapes=[pltpu.VMEM((tt_rows // 8, 8, cb), F32), pltpu.VMEM((tt_rows // 8, 8, cb), F32),
                        pltpu.VMEM((8, cb), F32)],
        compiler_params=_cparams("arbitrary", "arbitrary", "arbitrary"),
        name="rglru_bwd" if reverse else "rglru_fwd",
    )(*args)


def _ffn_kernel(x1_ref, xp_ref, xn_ref, g_ref, wv_ref, wg_ref, cwv_ref, cwg_ref, cbv_ref, cbg_ref, wd_ref,
                x2_ref, x2b_ref, ext_scr, u_scr, act_scr, *, n_tt, n_fc, tm, sub, nsplit):
    halo = FFN_HALO
    tt = pl.program_id(1)
    j = pl.program_id(2)
    fc = act_scr.shape[2]
    n_sub = fc // sub
    n_lt = sub // LANES
    d = x2_ref.shape[1]

    def up_piece(slot, s, k):
        def run():
            w_ref = (wv_ref, wg_ref)[k]
            u = jnp.dot(ext_scr[...], w_ref[:, s * sub:(s + 1) * sub], preferred_element_type=F32)
            for c in range(n_lt):
                u_scr[slot, k, s * n_lt + c] = u[:, c * LANES:(c + 1) * LANES]
        return run

    def conv(slot, k, c, r0, cw_ref, cb_ref):
        cols = slice(c * LANES, (c + 1) * LANES)
        out = u_scr[slot, k, c, pl.ds(halo - 1 + r0, ACT_ROWS), :] * cw_ref[0:1, cols]
        out = out + u_scr[slot, k, c, pl.ds(halo + r0, ACT_ROWS), :] * cw_ref[1:2, cols]
        out = out + u_scr[slot, k, c, pl.ds(halo + 1 + r0, ACT_ROWS), :] * cw_ref[2:3, cols]
        return out + cb_ref[:, cols]

    def act_piece(slot, c):
        def run():
            for r0 in range(0, tm, ACT_ROWS):
                val = conv(slot, 0, c, r0, cwv_ref, cbv_ref)
                gate = conv(slot, 1, c, r0, cwg_ref, cbg_ref)
                act_scr[slot, r0:r0 + ACT_ROWS, c * LANES:(c + 1) * LANES] = (jax.nn.gelu(gate) * val).astype(BF16)
        return run

    def down_piece(slot, n):
        def run():
            ncols = slice(n * nsplit, (n + 1) * nsplit)
            x2_ref[:, ncols] += jnp.dot(act_scr[slot], wd_ref[:, ncols], preferred_element_type=F32)
        return run

    def up(slot):
        return [up_piece(slot, s, k) for s in range(n_sub) for k in range(2)]

    def activate(slot):
        return [act_piece(slot, c) for c in range(fc // LANES)]

    def down(slot):
        return [down_piece(slot, n) for n in range(d // nsplit)]

    @pl.when(j == 0)
    def _():
        g = g_ref[...]
        zero = jnp.zeros((halo, d), BF16)
        ext_scr[0:halo, :] = jnp.where(tt > 0, _rms_scale(xp_ref[...], g).astype(BF16), zero)
        for r0 in range(0, tm, ACT_ROWS):
            rows = x1_ref[r0:r0 + ACT_ROWS, :]
            x2_ref[r0:r0 + ACT_ROWS, :] = rows
            ext_scr[halo + r0:halo + r0 + ACT_ROWS, :] = _rms_scale(rows, g).astype(BF16)
        ext_scr[halo + tm:, :] = jnp.where(tt < n_tt - 1, _rms_scale(xn_ref[...], g).astype(BF16), zero)
        _run_interleaved(up(0), [])

    @pl.when(j == 1)
    def _():
        _run_interleaved(up(1), activate(0))

    for par in range(2):
        @pl.when((j >= 2) & (j < n_fc) & (lax.rem(j, 2) == par))
        def _(par=par):
            _run_interleaved(up(par), activate(1 - par))
            _run_interleaved(down(par), [])

    @pl.when(j == n_fc)
    def _():
        _run_interleaved(down(n_fc % 2), activate((n_fc - 1) % 2))

    @pl.when(j == n_fc + 1)
    def _():
        _run_interleaved(down((n_fc - 1) % 2), [])
        x2b_ref[...] = x2_ref[...].astype(BF16)


def _ffn(x1, g, w_up, conv_w, conv_b, w_down, batch, seq, tm=512, fc=512):
    m, d = x1.shape
    d_ff = w_down.shape[0]
    halo = FFN_HALO
    tm = _tile(seq, tm)
    fc = _tile(d_ff, fc)
    sub = _tile(fc, MXU_WIDTH)
    nsplit = _tile(d, 512)
    n_tt = seq // tm
    n_fc = d_ff // fc
    hb = tm // halo
    n_hb = seq // halo
    assert n_fc >= 2 and tm % ACT_ROWS == 0
    main = pl.BlockSpec((tm, d), lambda b, t, j: (b * n_tt + t, 0), pipeline_mode=pl.Buffered(1))
    prev = pl.BlockSpec((halo, d), lambda b, t, j: (b * n_hb + jnp.maximum(t * hb - 1, 0), 0))
    nxt = pl.BlockSpec((halo, d), lambda b, t, j: (b * n_hb + jnp.minimum((t + 1) * hb, n_hb - 1), 0))
    chunk = lambda j, lag: jnp.clip(j - lag, 0, n_fc - 1)
    wv = pl.BlockSpec((d, fc), lambda b, t, j: (0, chunk(j, 0)))
    wg = pl.BlockSpec((d, fc), lambda b, t, j: (0, n_fc + chunk(j, 0)))
    cv = lambda rows: pl.BlockSpec((rows, fc), lambda b, t, j: (0, chunk(j, 1)))
    cg = lambda rows: pl.BlockSpec((rows, fc), lambda b, t, j: (0, n_fc + chunk(j, 1)))
    wd = pl.BlockSpec((fc, d), lambda b, t, j: (chunk(j, 2), 0))
    cb2 = conv_b.reshape(1, -1).astype(F32)
    cw = conv_w.astype(F32)
    kern = functools.partial(_ffn_kernel, n_tt=n_tt, n_fc=n_fc, tm=tm, sub=sub, nsplit=nsplit)
    return pl.pallas_call(
        kern,
        grid=(batch, n_tt, n_fc + 2),
        in_specs=[main, prev, nxt, pl.BlockSpec((1, d), lambda b, t, j: (0, 0)),
                  wv, wg, cv(3), cg(3), cv(1), cg(1), wd],
        out_specs=[main, main],
        out_shape=[jax.ShapeDtypeStruct((m, d), F32), jax.ShapeDtypeStruct((m, d), BF16)],
        scratch_shapes=[pltpu.VMEM((tm + 2 * halo, d), BF16),
                        pltpu.VMEM((2, 2, fc // LANES, tm + 2 * halo, LANES), F32),
                        pltpu.VMEM((2, tm, fc), BF16)],
        compiler_params=_cparams("parallel", "arbitrary", "arbitrary"),
        name="norm_conv_mlp",
    )(x1, x1, x1, g.reshape(1, d).astype(F32), w_up, w_up, cw, cw, cb2, cb2, w_down)


def _ple_kernel(xb_ref, x2_ref, p_ref, wp_ref, gp_ref, wg_ref, gf_ref, o_ref, e_scr, *, n_j, tn):
    j = pl.program_id(1)
    d = o_ref.shape[1]

    @pl.when(j == 0)
    def _():
        pb = p_ref[...].astype(BF16)
        ssq = jnp.zeros((pb.shape[0], 1), F32)
        for c in range(n_j):
            pe = jnp.dot(pb, wp_ref[:, c * tn:(c + 1) * tn], preferred_element_type=F32)
            e_scr[:, c * tn:(c + 1) * tn] = pe
            ssq = ssq + jnp.sum(pe * pe, axis=-1, keepdims=True)
        e_scr[...] = e_scr[...] * lax.rsqrt(ssq * (1.0 / d) + EPS) * gp_ref[...]

    gate = _sigmoid(jnp.dot(xb_ref[...], wg_ref[...], preferred_element_type=F32))
    for jj in range(n_j):
        @pl.when(j == jj)
        def _(jj=jj):
            o_ref[:, jj * tn:(jj + 1) * tn] = x2_ref[...] + gate * e_scr[:, jj * tn:(jj + 1) * tn]

    @pl.when(j == n_j - 1)
    def _():
        o_ref[...] = _rms_scale(o_ref[...], gf_ref[...])


def _ple_final(x2b, x2, p, w_ple, g_ple, w_gate, g_final, tm=512, tn=512):
    m, d = x2.shape
    kp = p.shape[1]
    tm = _tile(m, tm)
    tn = _tile(d, tn)
    n_j = d // tn
    row = lambda w: pl.BlockSpec((tm, w), lambda i, j: (i, 0))
    full = lambda r: pl.BlockSpec((r, d), lambda i, j: (0, 0))
    return pl.pallas_call(
        functools.partial(_ple_kernel, n_j=n_j, tn=tn),
        grid=(m // tm, n_j),
        in_specs=[row(d), pl.BlockSpec((tm, tn), lambda i, j: (i, j)), row(kp), full(kp), full(1),
                  pl.BlockSpec((d, tn), lambda i, j: (0, j)), full(1)],
        out_specs=row(d),
        out_shape=jax.ShapeDtypeStruct((m, d), F32),
        scratch_shapes=[pltpu.VMEM((tm, d), F32)],
        compiler_params=_cparams("parallel", "arbitrary"),
        name="ple_final",
    )(x2b, x2, p, w_ple, g_ple.reshape(1, d).astype(F32), w_gate, g_final.reshape(1, d).astype(F32))


def _encoder(x3, p4, wts):
    batch, seq, d = x3.shape
    m = batch * seq
    x = x3.reshape(m, d)
    n_q = wts["attn_sink"].shape[-1]
    attn_w = wts["w_o_attn"].shape[-2]
    d_rnn = wts["w_o_rnn"].shape[-2]
    hd = attn_w // n_q
    kv_w = (wts["w_in"].shape[-1] - attn_w - 2 * d_rnn) // 2
    n_kv = kv_w // hd
    qkv_w = attn_w + 2 * kv_w
    depth = wts["w_in"].shape[0]
    assert depth == 1, "the final norm is fused into the (single) layer's embedding kernel"
    ident = lambda acc: acc

    for l in range(depth):
        w_in = wts["w_in"][l]
        h = _rmsnorm(x, wts["g_mix"][l], BF16)
        qkv = _mm(h, w_in[:, :qkv_w].astype(BF16), ident, BF16, "in_qkv")
        xr = _mm(h, w_in[:, qkv_w:qkv_w + d_rnn].astype(BF16), ident, F32, "in_xr")
        gy = _mm(h, w_in[:, qkv_w + d_rnn:].astype(BF16), jax.nn.gelu, F32, "in_gelu_y")
        gates = _mm(h, wts["w_merge_gate"][l].astype(BF16), _sigmoid, F32, "merge_gates")

        attn = _attention(qkv, wts["attn_sink"][l], batch, seq, n_q, n_kv, hd)

        lru = (wts["rnn_conv_w"][l], wts["rnn_conv_b"][l], wts["lru_wa"][l], wts["lru_wx"][l],
               wts["lru_ba"][l], wts["lru_bx"][l], wts["lru_lambda"][l], batch, seq)
        h_fwd, xc = _rnn_direction(0, xr, *lru)
        rec = _rnn_direction(1, xc, *lru, hf=h_fwd, gy=gy)

        merged = _mm_merge(attn, rec, wts["w_o_attn"][l].astype(BF16), wts["w_o_rnn"][l].astype(BF16), gates, d)
        x1 = _mm(merged, wts["w_out"][l].astype(BF16), lambda acc, res: res + acc, F32, "out_residual", extra=x)
        x2, x2b = _ffn(x1, wts["g_ffn"][l], wts["w_up"][l].astype(BF16), wts["ffn_conv_w"][l],
                       wts["ffn_conv_b"][l], wts["w_down"][l].astype(BF16), batch, seq)
        y = _ple_final(x2b, x2, p4[l].reshape(m, -1), wts["w_ple"][l].astype(BF16), wts["g_ple"][l],
                       wts["w_ple_gate"][l].astype(BF16), wts["g_final"])
    return y.reshape(batch, seq, d)


def kernel(x_prompt, x_sample, p_prompt, p_sample, g_mix, w_in, rnn_conv_w, rnn_conv_b, lru_wa, lru_ba, lru_wx, lru_bx, lru_lambda, attn_sink, w_o_attn, w_o_rnn, w_merge_gate, w_out, g_ffn, w_up, ffn_conv_w, ffn_conv_b, w_down, w_ple_gate, w_ple, g_ple, g_final):
    wts = dict(g_mix=g_mix, w_in=w_in, rnn_conv_w=rnn_conv_w, rnn_conv_b=rnn_conv_b, lru_wa=lru_wa, lru_ba=lru_ba,
               lru_wx=lru_wx, lru_bx=lru_bx, lru_lambda=lru_lambda, attn_sink=attn_sink, w_o_attn=w_o_attn,
               w_o_rnn=w_o_rnn, w_merge_gate=w_merge_gate, w_out=w_out, g_ffn=g_ffn, w_up=w_up,
               ffn_conv_w=ffn_conv_w, ffn_conv_b=ffn_conv_b, w_down=w_down, w_ple_gate=w_ple_gate, w_ple=w_ple,
               g_ple=g_ple, g_final=g_final)
    return (_encoder(x_prompt, p_prompt, wts), _encoder(x_sample, p_sample, wts))
```

```python
import functools

import numpy as np
import jax
import jax.numpy as jnp
from jax import lax
from jax.experimental import pallas as pl
from jax.experimental.pallas import tpu as pltpu

F32 = jnp.float32
BF16 = jnp.bfloat16

EPS = 1e-6
NEG = -1e30
ATTN_BLOCK = 128
LRU_C = 8.0
RNN_HALO = 8
FFN_HALO = 16
LANES = 128
ACT_ROWS = 64
MXU_WIDTH = 256
V7X_VMEM_LIMIT_BYTES = 60 * 1024 * 1024


def _cparams(*sem):
    return pltpu.CompilerParams(dimension_semantics=sem, vmem_limit_bytes=V7X_VMEM_LIMIT_BYTES)


def _tile(dim, pref):
    t = min(dim, pref)
    while dim % t:
        t //= 2
    return t


def _run_interleaved(mxu_pieces, vpu_pieces):
    n_m, n_v = len(mxu_pieces), len(vpu_pieces)
    done_v = 0
    for i, piece in enumerate(mxu_pieces):
        piece()
        upto = (i + 1) * n_v // n_m
        for v in vpu_pieces[done_v:upto]:
            v()
        done_v = max(done_v, upto)
    for v in vpu_pieces[done_v:]:
        v()


def _sigmoid(x):
    return 0.5 * jnp.tanh(0.5 * x) + 0.5


def _rms_scale(x, g):
    ms = jnp.mean(x * x, axis=-1, keepdims=True)
    return x * lax.rsqrt(ms + EPS) * g


def _rmsnorm_kernel(x_ref, g_ref, o_ref):
    o_ref[...] = _rms_scale(x_ref[...].astype(F32), g_ref[...]).astype(o_ref.dtype)


def _rmsnorm(x, g, out_dtype, tm=256):
    m, d = x.shape
    tm = _tile(m, tm)
    return pl.pallas_call(
        _rmsnorm_kernel,
        grid=(m // tm,),
        in_specs=[pl.BlockSpec((tm, d), lambda i: (i, 0)), pl.BlockSpec((1, d), lambda i: (0, 0))],
        out_specs=pl.BlockSpec((tm, d), lambda i: (i, 0)),
        out_shape=jax.ShapeDtypeStruct((m, d), out_dtype),
        compiler_params=_cparams("parallel"),
        name="rmsnorm",
    )(x, g.reshape(1, d).astype(F32))


def _mm_kernel(*refs, epilogue):
    a_ref, b_ref, o_ref = refs[0], refs[1], refs[-1]
    acc = jnp.dot(a_ref[...], b_ref[...], preferred_element_type=F32)
    o_ref[...] = epilogue(acc, *[r[...] for r in refs[2:-1]]).astype(o_ref.dtype)


def _mm(a, b, epilogue, out_dtype, name, extra=None, tm=1024, tn=1024):
    m, k = a.shape
    n = b.shape[1]
    tm = _tile(m, tm)
    tn = _tile(n, tn)
    tile = pl.BlockSpec((tm, tn), lambda i, j: (i, j))
    extras = [] if extra is None else [extra]
    return pl.pallas_call(
        functools.partial(_mm_kernel, epilogue=epilogue),
        grid=(m // tm, n // tn),
        in_specs=[pl.BlockSpec((tm, k), lambda i, j: (i, 0)), pl.BlockSpec((k, tn), lambda i, j: (0, j))]
        + [tile] * len(extras),
        out_specs=tile,
        out_shape=jax.ShapeDtypeStruct((m, n), out_dtype),
        compiler_params=_cparams("parallel", "arbitrary"),
        name=name,
    )(a, b, *extras)


def _merge_kernel(a_ref, r_ref, wa_ref, wr_ref, ga_ref, gr_ref, o_ref):
    pa = jnp.dot(a_ref[...], wa_ref[...], preferred_element_type=F32)
    pr = jnp.dot(r_ref[...], wr_ref[...], preferred_element_type=F32)
    o_ref[...] = (ga_ref[...] * pa + gr_ref[...] * pr).astype(o_ref.dtype)


def _mm_merge(attn, rec, w_oa, w_or, gates, d, tm=1024, tn=512):
    m = attn.shape[0]
    tm = _tile(m, tm)
    tn = _tile(d, tn)
    n_j = d // tn
    return pl.pallas_call(
        _merge_kernel,
        grid=(m // tm, n_j),
        in_specs=[pl.BlockSpec((tm, attn.shape[1]), lambda i, j: (i, 0)),
                  pl.BlockSpec((tm, rec.shape[1]), lambda i, j: (i, 0)),
                  pl.BlockSpec((w_oa.shape[0], tn), lambda i, j: (0, j)),
                  pl.BlockSpec((w_or.shape[0], tn), lambda i, j: (0, j)),
                  pl.BlockSpec((tm, tn), lambda i, j: (i, j)),
                  pl.BlockSpec((tm, tn), lambda i, j: (i, n_j + j))],
        out_specs=pl.BlockSpec((tm, tn), lambda i, j: (i, j)),
        out_shape=jax.ShapeDtypeStruct((m, d), BF16),
        compiler_params=_cparams("parallel", "arbitrary"),
        name="merge",
    )(attn, rec, w_oa, w_or, gates, gates)


def _attn_kernel(sink_ref, q_ref, kp_ref, kc_ref, kn_ref, vp_ref, vc_ref, vn_ref, o_ref, bias_scr,
                 *, n_steps, n_kv, group, hd, slopes):
    blk = ATTN_BLOCK
    qr = q_ref.shape[0]
    nk = qr + 2 * blk
    n = pl.program_id(1)

    @pl.when((pl.program_id(0) == 0) & (n == 0))
    def _():
        t = lax.broadcasted_iota(jnp.int32, (qr, nk), 0)
        s = lax.broadcasted_iota(jnp.int32, (qr, nk), 1)
        dist = jnp.abs(s - blk - t)
        distf = dist.astype(F32)
        for hq in range(n_kv * group):
            bias_scr[hq * qr:(hq + 1) * qr, :] = jnp.where(dist <= blk, -slopes[hq] * distf, NEG)

    col = lax.broadcasted_iota(jnp.int32, (1, nk), 1)
    edge = jnp.where(((col < blk) & (n == 0)) | ((col >= blk + qr) & (n == n_steps - 1)), NEG, 0.0).astype(F32)
    scale = hd ** -0.5
    n_q = n_kv * group

    q_rows = [q_ref[:, hq * hd:(hq + 1) * hd] for hq in range(n_q)]
    scores = []
    for h in range(n_kv):
        q_h = jnp.concatenate(q_rows[h * group:(h + 1) * group], axis=0)
        k_h = jnp.concatenate([r[:, h * hd:(h + 1) * hd] for r in (kp_ref, kc_ref, kn_ref)], axis=0)
        scores.append(lax.dot_general(q_h, k_h, (((1,), (1,)), ((), ())), preferred_element_type=F32))
    s = jnp.concatenate(scores, axis=0) * scale + bias_scr[...] + edge
    sk = jnp.concatenate([jnp.full((qr, 1), sink_ref[hq], F32) for hq in range(n_q)], axis=0)
    m = jnp.maximum(jnp.max(s, axis=-1, keepdims=True), sk)
    e = jnp.exp(s - m)
    inv = 1.0 / (jnp.sum(e, axis=-1, keepdims=True) + jnp.exp(sk - m))
    eb = e.astype(BF16)
    rows = group * qr
    for h in range(n_kv):
        v_h = jnp.concatenate([r[:, h * hd:(h + 1) * hd] for r in (vp_ref, vc_ref, vn_ref)], axis=0)
        o = jnp.dot(eb[h * rows:(h + 1) * rows], v_h, preferred_element_type=F32) * inv[h * rows:(h + 1) * rows]
        for g in range(group):
            o_ref[:, (h * group + g) * hd:(h * group + g + 1) * hd] = o[g * qr:(g + 1) * qr].astype(o_ref.dtype)


def _attention(qkv, sink, batch, seq, n_q, n_kv, hd, q_blocks=1):
    blk = ATTN_BLOCK
    qr = q_blocks * blk
    nb = seq // blk
    n_steps = seq // qr
    attn_w, kv_w = n_q * hd, n_kv * hd
    assert attn_w % kv_w == 0 and seq % qr == 0
    kcol = attn_w // kv_w
    slopes = tuple(float(v) for v in (2.0 ** (-8.0 * (np.arange(n_q) + 1) / n_q)).astype(np.float32))

    def cur(w, colblk):
        return pl.BlockSpec((qr, w), lambda b, n: (b * n_steps + n, colblk))

    def side(colblk, first_blk):
        return pl.BlockSpec((blk, kv_w),
                            lambda b, n: (b * nb + jnp.clip(n * q_blocks + first_blk, 0, nb - 1), colblk))

    kern = functools.partial(_attn_kernel, n_steps=n_steps, n_kv=n_kv, group=n_q // n_kv, hd=hd, slopes=slopes)
    return pl.pallas_call(
        kern,
        grid=(batch, n_steps),
        in_specs=[pl.BlockSpec(memory_space=pltpu.SMEM),
                  cur(attn_w, 0),
                  side(kcol, -1), cur(kv_w, kcol), side(kcol, q_blocks),
                  side(kcol + 1, -1), cur(kv_w, kcol + 1), side(kcol + 1, q_blocks)],
        out_specs=cur(attn_w, 0),
        out_shape=jax.ShapeDtypeStruct((batch * seq, attn_w), BF16),
        scratch_shapes=[pltpu.VMEM((n_q * qr, qr + 2 * blk), F32)],
        compiler_params=_cparams("arbitrary", "arbitrary"),
        name="banded_gqa",
    )(sink.astype(F32), qkv, qkv, qkv, qkv, qkv, qkv, qkv)


def _lru_scan(a_scr, b_scr, carry_scr, n_slab, reverse):
    row = lax.broadcasted_iota(jnp.int32, a_scr.shape[1:], 0)

    def slab(i, carry):
        idx = (n_slab - 1 - i) if reverse else i
        a = a_scr[idx]
        b = b_scr[idx]
        for k in (1, 2, 4):
            keep = (row < 8 - k) if reverse else (row >= k)
            shift = (8 - k) if reverse else k
            a_sh = jnp.where(keep, pltpu.roll(a, shift, 0), 1.0)
            b_sh = jnp.where(keep, pltpu.roll(b, shift, 0), 0.0)
            b = b + a * b_sh
            a = a * a_sh
        h = a * carry + b
        b_scr[idx] = h
        return jnp.broadcast_to(h[0:1] if reverse else h[7:8], h.shape)

    carry_scr[...] = lax.fori_loop(0, n_slab, slab, carry_scr[...], unroll=16)


def _rnn_kernel(*refs, reverse, n_tt, tt_rows, n_sub, sub_w):
    if reverse:
        (xc_ref, wa_ref, wx_ref, ba_ref, bx_ref, lam_ref, hf_ref, gy_ref, o_ref, a_scr, b_scr, carry_scr) = refs
    else:
        (xm_ref, xp_ref, xn_ref, cw_ref, cb_ref, wa_ref, wx_ref, ba_ref, bx_ref, lam_ref,
         o_ref, xc_ref, a_scr, b_scr, carry_scr) = refs
    halo = RNN_HALO
    tt = pl.program_id(2)

    @pl.when(tt == 0)
    def _():
        carry_scr[...] = jnp.zeros_like(carry_scr)

    if reverse:
        xc = xc_ref[...]
    else:
        xp = jnp.where(tt > 0, xp_ref[...], 0.0)
        xn = jnp.where(tt < n_tt - 1, xn_ref[...], 0.0)
        ext = jnp.concatenate([xp, xm_ref[...], xn], axis=0)
        n_ext = tt_rows + 2 * halo
        cw = cw_ref[...]
        xc = pltpu.roll(ext, 1, 0)[halo:halo + tt_rows] * cw[0:1]
        xc = xc + ext[halo:halo + tt_rows] * cw[1:2]
        xc = xc + pltpu.roll(ext, n_ext - 1, 0)[halo:halo + tt_rows] * cw[2:3]
        xc = xc + pltpu.roll(ext, n_ext - 2, 0)[halo:halo + tt_rows] * cw[3:4]
        xc = xc + cb_ref[...]
        xc_ref[...] = xc

    xcb = xc.astype(BF16)
    r_pre = jnp.concatenate(
        [jnp.dot(xcb[:, s * sub_w:(s + 1) * sub_w], wa_ref[s], preferred_element_type=F32) for s in range(n_sub)],
        axis=1)
    i_pre = jnp.concatenate(
        [jnp.dot(xcb[:, s * sub_w:(s + 1) * sub_w], wx_ref[s], preferred_element_type=F32) for s in range(n_sub)],
        axis=1)
    half_c_sp = (-0.5 * LRU_C) * jax.nn.softplus(-lam_ref[...])
    log_a = half_c_sp * jnp.tanh(r_pre + ba_ref[...]) + half_c_sp
    ig = 0.5 * jnp.tanh(i_pre + bx_ref[...]) + 0.5
    a = jnp.exp(log_a)
    one_minus_a2 = jnp.tanh(-log_a) * (1.0 + a * a)
    a_scr[...] = a.reshape(a_scr.shape)
    b_scr[...] = (jnp.sqrt(one_minus_a2) * (ig * xc)).reshape(b_scr.shape)

    _lru_scan(a_scr, b_scr, carry_scr, tt_rows // 8, reverse)
    hs = b_scr[...].reshape(tt_rows, -1)
    if reverse:
        o_ref[...] = ((hf_ref[...] + hs) * gy_ref[...]).astype(o_ref.dtype)
    else:
        o_ref[...] = hs


def _rnn_direction(d, x_in, conv_w, conv_b, wa, wx, ba, bx, lam, batch, seq, hf=None, gy=None, tt_rows=512, cb=512):
    m, d_rnn = x_in.shape
    sub_w = wa.shape[2]
    reverse = d == 1
    halo = RNN_HALO
    tt_rows = _tile(seq, tt_rows)
    cb = max(_tile(d_rnn, cb), sub_w)
    n_sub = cb // sub_w
    n_tt = seq // tt_rows
    hb = tt_rows // halo
    n_hb = seq // halo

    def te(t):
        return (n_tt - 1 - t) if reverse else t

    main = pl.BlockSpec((tt_rows, cb), lambda b, c, t: (b * n_tt + te(t), c))
    vec = lambda rows: pl.BlockSpec((rows, cb), lambda b, c, t: (0, c))
    wspec = pl.BlockSpec((n_sub, sub_w, sub_w), lambda b, c, t: (c, 0, 0))
    gate_specs = [wspec, wspec, vec(1), vec(1), vec(1)]
    gate_args = [(0.5 * wa[d]).astype(BF16), (0.5 * wx[d]).astype(BF16),
                 (0.5 * ba[d]).reshape(1, -1).astype(F32), (0.5 * bx[d]).reshape(1, -1).astype(F32),
                 lam[d].reshape(1, -1).astype(F32)]
    if reverse:
        in_specs = [main] + gate_specs + [main, main]
        args = [x_in] + gate_args + [hf, gy]
        out_specs = main
        out_shape = jax.ShapeDtypeStruct((m, d_rnn), BF16)
    else:
        prev = pl.BlockSpec((halo, cb), lambda b, c, t: (b * n_hb + jnp.maximum(t * hb - 1, 0), c))
        nxt = pl.BlockSpec((halo, cb), lambda b, c, t: (b * n_hb + jnp.minimum((t + 1) * hb, n_hb - 1), c))
        in_specs = [main, prev, nxt, vec(4), vec(1)] + gate_specs
        args = [x_in, x_in, x_in, conv_w.astype(F32), conv_b.reshape(1, -1).astype(F32)] + gate_args
        out_specs = [main, main]
        out_shape = [jax.ShapeDtypeStruct((m, d_rnn), F32), jax.ShapeDtypeStruct((m, d_rnn), F32)]
    kern = functools.partial(_rnn_kernel, reverse=reverse, n_tt=n_tt, tt_rows=tt_rows, n_sub=n_sub, sub_w=sub_w)
    return pl.pallas_call(
        kern,
        grid=(batch, d_rnn // cb, n_tt),
        in_specs=in_specs,
        out_specs=out_specs,
        out_shape=out_shape,
        scratch_shapes=[pltpu.VMEM((tt_rows // 8, 8, cb), F32), pltpu.VMEM((tt_rows // 8, 8, cb), F32),
                        pltpu.VMEM((8, cb), F32)],
        compiler_params=_cparams("arbitrary", "arbitrary", "arbitrary"),
        name="rglru_bwd" if reverse else "rglru_fwd",
    )(*args)


def _ffn_kernel(x1_ref, xp_ref, xn_ref, g_ref, wv_ref, wg_ref, cwv_ref, cwg_ref, cbv_ref, cbg_ref, wd_ref,
                x2_ref, x2b_ref, ext_scr, u_scr, act_scr, *, n_tt, n_fc, tm, sub, nsplit):
    halo = FFN_HALO
    tt = pl.program_id(1)
    j = pl.program_id(2)
    fc = act_scr.shape[2]
    n_sub = fc // sub
    n_lt = sub // LANES
    d = x2_ref.shape[1]

    def up_piece(slot, s, k):
        def run():
            w_ref = (wv_ref, wg_ref)[k]
            u = jnp.dot(ext_scr[...], w_ref[:, s * sub:(s + 1) * sub], preferred_element_type=F32)
            for c in range(n_lt):
                u_scr[slot, k, s * n_lt + c] = u[:, c * LANES:(c + 1) * LANES]
        return run

    def conv(slot, k, c, r0, cw_ref, cb_ref):
        cols = slice(c * LANES, (c + 1) * LANES)
        out = u_scr[slot, k, c, pl.ds(halo - 1 + r0, ACT_ROWS), :] * cw_ref[0:1, cols]
        out = out + u_scr[slot, k, c, pl.ds(halo + r0, ACT_ROWS), :] * cw_ref[1:2, cols]
        out = out + u_scr[slot, k, c, pl.ds(halo + 1 + r0, ACT_ROWS), :] * cw_ref[2:3, cols]
        return out + cb_ref[:, cols]

    def act_piece(slot, c):
        def run():
            for r0 in range(0, tm, ACT_ROWS):
                val = conv(slot, 0, c, r0, cwv_ref, cbv_ref)
                gate = conv(slot, 1, c, r0, cwg_ref, cbg_ref)
                act_scr[slot, r0:r0 + ACT_ROWS, c * LANES:(c + 1) * LANES] = (jax.nn.gelu(gate) * val).astype(BF16)
        return run

    def down_piece(slot, n):
        def run():
            ncols = slice(n * nsplit, (n + 1) * nsplit)
            x2_ref[:, ncols] += jnp.dot(act_scr[slot], wd_ref[:, ncols], preferred_element_type=F32)
        return run

    def up(slot):
        return [up_piece(slot, s, k) for s in range(n_sub) for k in range(2)]

    def activate(slot):
        return [act_piece(slot, c) for c in range(fc // LANES)]

    def down(slot):
        return [down_piece(slot, n) for n in range(d // nsplit)]

    @pl.when(j == 0)
    def _():
        g = g_ref[...]
        zero = jnp.zeros((halo, d), BF16)
        ext_scr[0:halo, :] = jnp.where(tt > 0, _rms_scale(xp_ref[...], g).astype(BF16), zero)
        for r0 in range(0, tm, ACT_ROWS):
            rows = x1_ref[r0:r0 + ACT_ROWS, :]
            x2_ref[r0:r0 + ACT_ROWS, :] = rows
            ext_scr[halo + r0:halo + r0 + ACT_ROWS, :] = _rms_scale(rows, g).astype(BF16)
        ext_scr[halo + tm:, :] = jnp.where(tt < n_tt - 1, _rms_scale(xn_ref[...], g).astype(BF16), zero)
        _run_interleaved(up(0), [])

    @pl.when(j == 1)
    def _():
        _run_interleaved(up(1), activate(0))

    for par in range(2):
        @pl.when((j >= 2) & (j < n_fc) & (lax.rem(j, 2) == par))
        def _(par=par):
            _run_interleaved(up(par), activate(1 - par))
            _run_interleaved(down(par), [])

    @pl.when(j == n_fc)
    def _():
        _run_interleaved(down(n_fc % 2), activate((n_fc - 1) % 2))

    @pl.when(j == n_fc + 1)
    def _():
        _run_interleaved(down((n_fc - 1) % 2), [])
        x2b_ref[...] = x2_ref[...].astype(BF16)


def _ffn(x1, g, w_up, conv_w, conv_b, w_down, batch, seq, tm=512, fc=512):
    m, d = x1.shape
    d_ff = w_down.shape[0]
    halo = FFN_HALO
    tm = _tile(seq, tm)
    fc = _tile(d_ff, fc)
    sub = _tile(fc, MXU_WIDTH)
    nsplit = _tile(d, 512)
    n_tt = seq // tm
    n_fc = d_ff // fc
    hb = tm // halo
    n_hb = seq // halo
    assert n_fc >= 2 and tm % ACT_ROWS == 0
    main = pl.BlockSpec((tm, d), lambda b, t, j: (b * n_tt + t, 0), pipeline_mode=pl.Buffered(1))
    prev = pl.BlockSpec((halo, d), lambda b, t, j: (b * n_hb + jnp.maximum(t * hb - 1, 0), 0))
    nxt = pl.BlockSpec((halo, d), lambda b, t, j: (b * n_hb + jnp.minimum((t + 1) * hb, n_hb - 1), 0))
    chunk = lambda j, lag: jnp.clip(j - lag, 0, n_fc - 1)
    wv = pl.BlockSpec((d, fc), lambda b, t, j: (0, chunk(j, 0)))
    wg = pl.BlockSpec((d, fc), lambda b, t, j: (0, n_fc + chunk(j, 0)))
    cv = lambda rows: pl.BlockSpec((rows, fc), lambda b, t, j: (0, chunk(j, 1)))
    cg = lambda rows: pl.BlockSpec((rows, fc), lambda b, t, j: (0, n_fc + chunk(j, 1)))
    wd = pl.BlockSpec((fc, d), lambda b, t, j: (chunk(j, 2), 0))
    cb2 = conv_b.reshape(1, -1).astype(F32)
    cw = conv_w.astype(F32)
    kern = functools.partial(_ffn_kernel, n_tt=n_tt, n_fc=n_fc, tm=tm, sub=sub, nsplit=nsplit)
    return pl.pallas_call(
        kern,
        grid=(batch, n_tt, n_fc + 2),
        in_specs=[main, prev, nxt, pl.BlockSpec((1, d), lambda b, t, j: (0, 0)),
                  wv, wg, cv(3), cg(3), cv(1), cg(1), wd],
        out_specs=[main, main],
        out_shape=[jax.ShapeDtypeStruct((m, d), F32), jax.ShapeDtypeStruct((m, d), BF16)],
        scratch_shapes=[pltpu.VMEM((tm + 2 * halo, d), BF16),
                        pltpu.VMEM((2, 2, fc // LANES, tm + 2 * halo, LANES), F32),
                        pltpu.VMEM((2, tm, fc), BF16)],
        compiler_params=_cparams("parallel", "arbitrary", "arbitrary"),
        name="norm_conv_mlp",
    )(x1, x1, x1, g.reshape(1, d).astype(F32), w_up, w_up, cw, cw, cb2, cb2, w_down)


def _ple_kernel(xb_ref, x2_ref, p_ref, wp_ref, gp_ref, wg_ref, gf_ref, o_ref, e_scr, *, n_j, tn):
    j = pl.program_id(1)
    d = o_ref.shape[1]

    @pl.when(j == 0)
    def _():
        pb = p_ref[...].astype(BF16)
        ssq = jnp.zeros((pb.shape[0], 1), F32)
        for c in range(n_j):
            pe = jnp.dot(pb, wp_ref[:, c * tn:(c + 1) * tn], preferred_element_type=F32)
            e_scr[:, c * tn:(c + 1) * tn] = pe
            ssq = ssq + jnp.sum(pe * pe, axis=-1, keepdims=True)
        e_scr[...] = e_scr[...] * lax.rsqrt(ssq * (1.0 / d) + EPS) * gp_ref[...]

    gate = _sigmoid(jnp.dot(xb_ref[...], wg_ref[...], preferred_element_type=F32))
    for jj in range(n_j):
        @pl.when(j == jj)
        def _(jj=jj):
            o_ref[:, jj * tn:(jj + 1) * tn] = x2_ref[...] + gate * e_scr[:, jj * tn:(jj + 1) * tn]

    @pl.when(j == n_j - 1)
    def _():
        o_ref[...] = _rms_scale(o_ref[...], gf_ref[...])


def _ple_final(x2b, x2, p, w_ple, g_ple, w_gate, g_final, tm=512, tn=512):
    m, d = x2.shape
    kp = p.shape[1]
    tm = _tile(m, tm)
    tn = _tile(d, tn)
    n_j = d // tn
    row = lambda w: pl.BlockSpec((tm, w), lambda i, j: (i, 0))
    full = lambda r: pl.BlockSpec((r, d), lambda i, j: (0, 0))
    return pl.pallas_call(
        functools.partial(_ple_kernel, n_j=n_j, tn=tn),
        grid=(m // tm, n_j),
        in_specs=[row(d), pl.BlockSpec((tm, tn), lambda i, j: (i, j)), row(kp), full(kp), full(1),
                  pl.BlockSpec((d, tn), lambda i, j: (0, j)), full(1)],
        out_specs=row(d),
        out_shape=jax.ShapeDtypeStruct((m, d), F32),
        scratch_shapes=[pltpu.VMEM((tm, d), F32)],
        compiler_params=_cparams("parallel", "arbitrary"),
        name="ple_final",
    )(x2b, x2, p, w_ple, g_ple.reshape(1, d).astype(F32), w_gate, g_final.reshape(1, d).astype(F32))


def _encoder(x3, p4, wts):
    batch, seq, d = x3.shape
    m = batch * seq
    x = x3.reshape(m, d)
    n_q = wts["attn_sink"].shape[-1]
    attn_w = wts["w_o_attn"].shape[-2]
    d_rnn = wts["w_o_rnn"].shape[-2]
    hd = attn_w // n_q
    kv_w = (wts["w_in"].shape[-1] - attn_w - 2 * d_rnn) // 2
    n_kv = kv_w // hd
    qkv_w = attn_w + 2 * kv_w
    depth = wts["w_in"].shape[0]
    assert depth == 1, "the final norm is fused into the (single) layer's embedding kernel"
    ident = lambda acc: acc

    for l in range(depth):
        w_in = wts["w_in"][l]
        h = _rmsnorm(x, wts["g_mix"][l], BF16)
        qkv = _mm(h, w_in[:, :qkv_w].astype(BF16), ident, BF16, "in_qkv")
        xr = _mm(h, w_in[:, qkv_w:qkv_w + d_rnn].astype(BF16), ident, F32, "in_xr")
        gy = _mm(h, w_in[:, qkv_w + d_rnn:].astype(BF16), jax.nn.gelu, F32, "in_gelu_y")
        gates = _mm(h, wts["w_merge_gate"][l].astype(BF16), _sigmoid, F32, "merge_gates")

        attn = _attention(qkv, wts["attn_sink"][l], batch, seq, n_q, n_kv, hd)

        lru = (wts["rnn_conv_w"][l], wts["rnn_conv_b"][l], wts["lru_wa"][l], wts["lru_wx"][l],
               wts["lru_ba"][l], wts["lru_bx"][l], wts["lru_lambda"][l], batch, seq)
        h_fwd, xc = _rnn_direction(0, xr, *lru)
        rec = _rnn_direction(1, xc, *lru, hf=h_fwd, gy=gy)

        merged = _mm_merge(attn, rec, wts["w_o_attn"][l].astype(BF16), wts["w_o_rnn"][l].astype(BF16), gates, d)
        x1 = _mm(merged, wts["w_out"][l].astype(BF16), lambda acc, res: res + acc, F32, "out_residual", extra=x)
        x2, x2b = _ffn(x1, wts["g_ffn"][l], wts["w_up"][l].astype(BF16), wts["ffn_conv_w"][l],
                       wts["ffn_conv_b"][l], wts["w_down"][l].astype(BF16), batch, seq)
        y = _ple_final(x2b, x2, p4[l].reshape(m, -1), wts["w_ple"][l].astype(BF16), wts["g_ple"][l],
                       wts["w_ple_gate"][l].astype(BF16), wts["g_final"])
    return y.reshape(batch, seq, d)


def kernel(x_prompt, x_sample, p_prompt, p_sample, g_mix, w_in, rnn_conv_w, rnn_conv_b, lru_wa, lru_ba, lru_wx, lru_bx, lru_lambda, attn_sink, w_o_attn, w_o_rnn, w_merge_gate, w_out, g_ffn, w_up, ffn_conv_w, ffn_conv_b, w_down, w_ple_gate, w_ple, g_ple, g_final):
    wts = dict(g_mix=g_mix, w_in=w_in, rnn_conv_w=rnn_conv_w, rnn_conv_b=rnn_conv_b, lru_wa=lru_wa, lru_ba=lru_ba,
               lru_wx=lru_wx, lru_bx=lru_bx, lru_lambda=lru_lambda, attn_sink=attn_sink, w_o_attn=w_o_attn,
               w_o_rnn=w_o_rnn, w_merge_gate=w_merge_gate, w_out=w_out, g_ffn=g_ffn, w_up=w_up,
               ffn_conv_w=ffn_conv_w, ffn_conv_b=ffn_conv_b, w_down=w_down, w_ple_gate=w_ple_gate, w_ple=w_ple,
               g_ple=g_ple, g_final=g_final)
    return (_encoder(x_prompt, p_prompt, wts), _encoder(x_sample, p_sample, wts))
```

```python
import functools

import numpy as np
import jax
import jax.numpy as jnp
from jax import lax
from jax.experimental import pallas as pl
from jax.experimental.pallas import tpu as pltpu

F32 = jnp.float32
BF16 = jnp.bfloat16

EPS = 1e-6
NEG = -1e30
ATTN_BLOCK = 128
LRU_C = 8.0
RNN_HALO = 8
FFN_HALO = 16
LANES = 128
ACT_ROWS = 64
MXU_WIDTH = 256
V7X_VMEM_LIMIT_BYTES = 60 * 1024 * 1024


def _cparams(*sem):
    return pltpu.CompilerParams(dimension_semantics=sem, vmem_limit_bytes=V7X_VMEM_LIMIT_BYTES)


def _tile(dim, pref):
    t = min(dim, pref)
    while dim % t:
        t //= 2
    return t


def _run_interleaved(mxu_pieces, vpu_pieces):
    n_m, n_v = len(mxu_pieces), len(vpu_pieces)
    done_v = 0
    for i, piece in enumerate(mxu_pieces):
        piece()
        upto = (i + 1) * n_v // n_m
        for v in vpu_pieces[done_v:upto]:
            v()
        done_v = max(done_v, upto)
    for v in vpu_pieces[done_v:]:
        v()


def _sigmoid(x):
    return 0.5 * jnp.tanh(0.5 * x) + 0.5


def _rms_scale(x, g):
    ms = jnp.mean(x * x, axis=-1, keepdims=True)
    return x * lax.rsqrt(ms + EPS) * g


def _rmsnorm_kernel(x_ref, g_ref, o_ref):
    o_ref[...] = _rms_scale(x_ref[...].astype(F32), g_ref[...]).astype(o_ref.dtype)


def _rmsnorm(x, g, out_dtype, tm=256):
    m, d = x.shape
    tm = _tile(m, tm)
    return pl.pallas_call(
        _rmsnorm_kernel,
        grid=(m // tm,),
        in_specs=[pl.BlockSpec((tm, d), lambda i: (i, 0)), pl.BlockSpec((1, d), lambda i: (0, 0))],
        out_specs=pl.BlockSpec((tm, d), lambda i: (i, 0)),
        out_shape=jax.ShapeDtypeStruct((m, d), out_dtype),
        compiler_params=_cparams("parallel"),
        name="rmsnorm",
    )(x, g.reshape(1, d).astype(F32))


def _mm_kernel(*refs, epilogue):
    a_ref, b_ref, o_ref = refs[0], refs[1], refs[-1]
    acc = jnp.dot(a_ref[...], b_ref[...], preferred_element_type=F32)
    o_ref[...] = epilogue(acc, *[r[...] for r in refs[2:-1]]).astype(o_ref.dtype)


def _mm(a, b, epilogue, out_dtype, name, extra=None, tm=1024, tn=1024):
    m, k = a.shape
    n = b.shape[1]
    tm = _tile(m, tm)
    tn = _tile(n, tn)
    tile = pl.BlockSpec((tm, tn), lambda i, j: (i, j))
    extras = [] if extra is None else [extra]
    return pl.pallas_call(
        functools.partial(_mm_kernel, epilogue=epilogue),
        grid=(m // tm, n // tn),
        in_specs=[pl.BlockSpec((tm, k), lambda i, j: (i, 0)), pl.BlockSpec((k, tn), lambda i, j: (0, j))]
        + [tile] * len(extras),
        out_specs=tile,
        out_shape=jax.ShapeDtypeStruct((m, n), out_dtype),
        compiler_params=_cparams("parallel", "arbitrary"),
        name=name,
    )(a, b, *extras)


def _merge_kernel(a_ref, r_ref, wa_ref, wr_ref, ga_ref, gr_ref, o_ref):
    pa = jnp.dot(a_ref[...], wa_ref[...], preferred_element_type=F32)
    pr = jnp.dot(r_ref[...], wr_ref[...], preferred_element_type=F32)
    o_ref[...] = (ga_ref[...] * pa + gr_ref[...] * pr).astype(o_ref.dtype)


def _mm_merge(attn, rec, w_oa, w_or, gates, d, tm=1024, tn=512):
    m = attn.shape[0]
    tm = _tile(m, tm)
    tn = _tile(d, tn)
    n_j = d // tn
    return pl.pallas_call(
        _merge_kernel,
        grid=(m // tm, n_j),
        in_specs=[pl.BlockSpec((tm, attn.shape[1]), lambda i, j: (i, 0)),
                  pl.BlockSpec((tm, rec.shape[1]), lambda i, j: (i, 0)),
                  pl.BlockSpec((w_oa.shape[0], tn), lambda i, j: (0, j)),
                  pl.BlockSpec((w_or.shape[0], tn), lambda i, j: (0, j)),
                  pl.BlockSpec((tm, tn), lambda i, j: (i, j)),
                  pl.BlockSpec((tm, tn), lambda i, j: (i, n_j + j))],
        out_specs=pl.BlockSpec((tm, tn), lambda i, j: (i, j)),
        out_shape=jax.ShapeDtypeStruct((m, d), BF16),
        compiler_params=_cparams("parallel", "arbitrary"),
        name="merge",
    )(attn, rec, w_oa, w_or, gates, gates)


def _attn_kernel(sink_ref, q_ref, kp_ref, kc_ref, kn_ref, vp_ref, vc_ref, vn_ref, o_ref, bias_scr,
                 *, n_steps, n_kv, group, hd, slopes):
    blk = ATTN_BLOCK
    qr = q_ref.shape[0]
    nk = qr + 2 * blk
    n = pl.program_id(1)

    @pl.when((pl.program_id(0) == 0) & (n == 0))
    def _():
        t = lax.broadcasted_iota(jnp.int32, (qr, nk), 0)
        s = lax.broadcasted_iota(jnp.int32, (qr, nk), 1)
        dist = jnp.abs(s - blk - t)
        distf = dist.astype(F32)
        for hq in range(n_kv * group):
            bias_scr[hq * qr:(hq + 1) * qr, :] = jnp.where(dist <= blk, -slopes[hq] * distf, NEG)

    col = lax.broadcasted_iota(jnp.int32, (1, nk), 1)
    edge = jnp.where(((col < blk) & (n == 0)) | ((col >= blk + qr) & (n == n_steps - 1)), NEG, 0.0).astype(F32)
    scale = hd ** -0.5
    n_q = n_kv * group

    q_rows = [q_ref[:, hq * hd:(hq + 1) * hd] for hq in range(n_q)]
    scores = []
    for h in range(n_kv):
        q_h = jnp.concatenate(q_rows[h * group:(h + 1) * group], axis=0)
        k_h = jnp.concatenate([r[:, h * hd:(h + 1) * hd] for r in (kp_ref, kc_ref, kn_ref)], axis=0)
        scores.append(lax.dot_general(q_h, k_h, (((1,), (1,)), ((), ())), preferred_element_type=F32))
    s = jnp.concatenate(scores, axis=0) * scale + bias_scr[...] + edge
    sk = jnp.concatenate([jnp.full((qr, 1), sink_ref[hq], F32) for hq in range(n_q)], axis=0)
    m = jnp.maximum(jnp.max(s, axis=-1, keepdims=True), sk)
    e = jnp.exp(s - m)
    inv = 1.0 / (jnp.sum(e, axis=-1, keepdims=True) + jnp.exp(sk - m))
    eb = e.astype(BF16)
    rows = group * qr
    for h in range(n_kv):
        v_h = jnp.concatenate([r[:, h * hd:(h + 1) * hd] for r in (vp_ref, vc_ref, vn_ref)], axis=0)
        o = jnp.dot(eb[h * rows:(h + 1) * rows], v_h, preferred_element_type=F32) * inv[h * rows:(h + 1) * rows]
        for g in range(group):
            o_ref[:, (h * group + g) * hd:(h * group + g + 1) * hd] = o[g * qr:(g + 1) * qr].astype(o_ref.dtype)


def _attention(qkv, sink, batch, seq, n_q, n_kv, hd, q_blocks=1):
    blk = ATTN_BLOCK
    qr = q_blocks * blk
    nb = seq // blk
    n_steps = seq // qr
    attn_w, kv_w = n_q * hd, n_kv * hd
    assert attn_w % kv_w == 0 and seq % qr == 0
    kcol = attn_w // kv_w
    slopes = tuple(float(v) for v in (2.0 ** (-8.0 * (np.arange(n_q) + 1) / n_q)).astype(np.float32))

    def cur(w, colblk):
        return pl.BlockSpec((qr, w), lambda b, n: (b * n_steps + n, colblk))

    def side(colblk, first_blk):
        return pl.BlockSpec((blk, kv_w),
                            lambda b, n: (b * nb + jnp.clip(n * q_blocks + first_blk, 0, nb - 1), colblk))

    kern = functools.partial(_attn_kernel, n_steps=n_steps, n_kv=n_kv, group=n_q // n_kv, hd=hd, slopes=slopes)
    return pl.pallas_call(
        kern,
        grid=(batch, n_steps),
        in_specs=[pl.BlockSpec(memory_space=pltpu.SMEM),
                  cur(attn_w, 0),
                  side(kcol, -1), cur(kv_w, kcol), side(kcol, q_blocks),
                  side(kcol + 1, -1), cur(kv_w, kcol + 1), side(kcol + 1, q_blocks)],
        out_specs=cur(attn_w, 0),
        out_shape=jax.ShapeDtypeStruct((batch * seq, attn_w), BF16),
        scratch_shapes=[pltpu.VMEM((n_q * qr, qr + 2 * blk), F32)],
        compiler_params=_cparams("arbitrary", "arbitrary"),
        name="banded_gqa",
    )(sink.astype(F32), qkv, qkv, qkv, qkv, qkv, qkv, qkv)


def _lru_scan(a_scr, b_scr, carry_scr, n_slab, reverse):
    row = lax.broadcasted_iota(jnp.int32, a_scr.shape[1:], 0)

    def slab(i, carry):
        idx = (n_slab - 1 - i) if reverse else i
        a = a_scr[idx]
        b = b_scr[idx]
        for k in (1, 2, 4):
            keep = (row < 8 - k) if reverse else (row >= k)
            shift = (8 - k) if reverse else k
            a_sh = jnp.where(keep, pltpu.roll(a, shift, 0), 1.0)
            b_sh = jnp.where(keep, pltpu.roll(b, shift, 0), 0.0)
            b = b + a * b_sh
            a = a * a_sh
        h = a * carry + b
        b_scr[idx] = h
        return jnp.broadcast_to(h[0:1] if reverse else h[7:8], h.shape)

    carry_scr[...] = lax.fori_loop(0, n_slab, slab, carry_scr[...], unroll=n_slab)


def _rnn_kernel(*refs, reverse, n_tt, tt_rows, n_sub, sub_w):
    if reverse:
        (xc_ref, wa_ref, wx_ref, ba_ref, bx_ref, lam_ref, hf_ref, gy_ref, o_ref, a_scr, b_scr, carry_scr) = refs
    else:
        (xm_ref, xp_ref, xn_ref, cw_ref, cb_ref, wa_ref, wx_ref, ba_ref, bx_ref, lam_ref,
         o_ref, xc_ref, a_scr, b_scr, carry_scr) = refs
    halo = RNN_HALO
    tt = pl.program_id(2)

    @pl.when(tt == 0)
    def _():
        carry_scr[...] = jnp.zeros_like(carry_scr)

    if reverse:
        xc = xc_ref[...]
    else:
        xp = jnp.where(tt > 0, xp_ref[...], 0.0)
        xn = jnp.where(tt < n_tt - 1, xn_ref[...], 0.0)
        ext = jnp.concatenate([xp, xm_ref[...], xn], axis=0)
        n_ext = tt_rows + 2 * halo
        cw = cw_ref[...]
        xc = pltpu.roll(ext, 1, 0)[halo:halo + tt_rows] * cw[0:1]
        xc = xc + ext[halo:halo + tt_rows] * cw[1:2]
        xc = xc + pltpu.roll(ext, n_ext - 1, 0)[halo:halo + tt_rows] * cw[2:3]
        xc = xc + pltpu.roll(ext, n_ext - 2, 0)[halo:halo + tt_rows] * cw[3:4]
        xc = xc + cb_ref[...]
        xc_ref[...] = xc

    xcb = xc.astype(BF16)
    r_pre = jnp.concatenate(
        [jnp.dot(xcb[:, s * sub_w:(s + 1) * sub_w], wa_ref[s], preferred_element_type=F32) for s in range(n_sub)],
        axis=1)
    i_pre = jnp.concatenate(
        [jnp.dot(xcb[:, s * sub_w:(s + 1) * sub_w], wx_ref[s], preferred_element_type=F32) for s in range(n_sub)],
        axis=1)
    half_c_sp = (-0.5 * LRU_C) * jax.nn.softplus(-lam_ref[...])
    log_a = half_c_sp * jnp.tanh(r_pre + ba_ref[...]) + half_c_sp
    ig = 0.5 * jnp.tanh(i_pre + bx_ref[...]) + 0.5
    a = jnp.exp(log_a)
    one_minus_a2 = jnp.tanh(-log_a) * (1.0 + a * a)
    a_scr[...] = a.reshape(a_scr.shape)
    b_scr[...] = (jnp.sqrt(one_minus_a2) * (ig * xc)).reshape(b_scr.shape)

    _lru_scan(a_scr, b_scr, carry_scr, tt_rows // 8, reverse)
    hs = b_scr[...].reshape(tt_rows, -1)
    if reverse:
        o_ref[...] = ((hf_ref[...] + hs) * gy_ref[...]).astype(o_ref.dtype)
    else:
        o_ref[...] = hs


def _rnn_direction(d, x_in, conv_w, conv_b, wa, wx, ba, bx, lam, batch, seq, hf=None, gy=None, tt_rows=512, cb=512):
    m, d_rnn = x_in.shape
    sub_w = wa.shape[2]
    reverse = d == 1
    halo = RNN_HALO
    tt_rows = _tile(seq, tt_rows)
    cb = max(_tile(d_rnn, cb), sub_w)
    n_sub = cb // sub_w
    n_tt = seq // tt_rows
    hb = tt_rows // halo
    n_hb = seq // halo

    def te(t):
        return (n_tt - 1 - t) if reverse else t

    main = pl.BlockSpec((tt_rows, cb), lambda b, c, t: (b * n_tt + te(t), c))
    vec = lambda rows: pl.BlockSpec((rows, cb), lambda b, c, t: (0, c))
    wspec = pl.BlockSpec((n_sub, sub_w, sub_w), lambda b, c, t: (c, 0, 0))
    gate_specs = [wspec, wspec, vec(1), vec(1), vec(1)]
    gate_args = [(0.5 * wa[d]).astype(BF16), (0.5 * wx[d]).astype(BF16),
                 (0.5 * ba[d]).reshape(1, -1).astype(F32), (0.5 * bx[d]).reshape(1, -1).astype(F32),
                 lam[d].reshape(1, -1).astype(F32)]
    if reverse:
        in_specs = [main] + gate_specs + [main, main]
        args = [x_in] + gate_args + [hf, gy]
        out_specs = main
        out_shape = jax.ShapeDtypeStruct((m, d_rnn), BF16)
    else:
        prev = pl.BlockSpec((halo, cb), lambda b, c, t: (b * n_hb + jnp.maximum(t * hb - 1, 0), c))
        nxt = pl.BlockSpec((halo, cb), lambda b, c, t: (b * n_hb + jnp.minimum((t + 1) * hb, n_hb - 1), c))
        in_specs = [main, prev, nxt, vec(4), vec(1)] + gate_specs
        args = [x_in, x_in, x_in, conv_w.astype(F32), conv_b.reshape(1, -1).astype(F32)] + gate_args
        out_specs = [main, main]
        out_shape = [jax.ShapeDtypeStruct((m, d_rnn), F32), jax.ShapeDtypeStruct((m, d_rnn), F32)]
    kern = functools.partial(_rnn_kernel, reverse=reverse, n_tt=n_tt, tt_rows=tt_rows, n_sub=n_sub, sub_w=sub_w)
    return pl.pallas_call(
        kern,
        grid=(batch, d_rnn // cb, n_tt),
        in_specs=in_specs,
        out_specs=out_specs,
        out_shape=out_shape,
        scratch_shapes=[pltpu.VMEM((tt_rows // 8, 8, cb), F32), pltpu.VMEM((tt_rows // 8, 8, cb), F32),
                        pltpu.VMEM((8, cb), F32)],
        compiler_params=_cparams("arbitrary", "arbitrary", "arbitrary"),
        name="rglru_bwd" if reverse else "rglru_fwd",
    )(*args)


def _ffn_kernel(x1_ref, xp_ref, xn_ref, g_ref, wv_ref, wg_ref, cwv_ref, cwg_ref, cbv_ref, cbg_ref, wd_ref,
                x2_ref, x2b_ref, ext_scr, u_scr, act_scr, *, n_tt, n_fc, tm, sub, nsplit):
    halo = FFN_HALO
    tt = pl.program_id(1)
    j = pl.program_id(2)
    fc = act_scr.shape[2]
    n_sub = fc // sub
    n_lt = sub // LANES
    d = x2_ref.shape[1]

    def up_piece(slot, s, k):
        def run():
            w_ref = (wv_ref, wg_ref)[k]
            u = jnp.dot(ext_scr[...], w_ref[:, s * sub:(s + 1) * sub], preferred_element_type=F32)
            for c in range(n_lt):
                u_scr[slot, k, s * n_lt + c] = u[:, c * LANES:(c + 1) * LANES]
        return run

    def conv(slot, k, c, r0, cw_ref, cb_ref):
        cols = slice(c * LANES, (c + 1) * LANES)
        out = u_scr[slot, k, c, pl.ds(halo - 1 + r0, ACT_ROWS), :] * cw_ref[0:1, cols]
        out = out + u_scr[slot, k, c, pl.ds(halo + r0, ACT_ROWS), :] * cw_ref[1:2, cols]
        out = out + u_scr[slot, k, c, pl.ds(halo + 1 + r0, ACT_ROWS), :] * cw_ref[2:3, cols]
        return out + cb_ref[:, cols]

    def act_piece(slot, c):
        def run():
            for r0 in range(0, tm, ACT_ROWS):
                val = conv(slot, 0, c, r0, cwv_ref, cbv_ref)
                gate = conv(slot, 1, c, r0, cwg_ref, cbg_ref)
                act_scr[slot, r0:r0 + ACT_ROWS, c * LANES:(c + 1) * LANES] = (jax.nn.gelu(gate) * val).astype(BF16)
        return run

    def down_piece(slot, n):
        def run():
            ncols = slice(n * nsplit, (n + 1) * nsplit)
            x2_ref[:, ncols] += jnp.dot(act_scr[slot], wd_ref[:, ncols], preferred_element_type=F32)
        return run

    def up(slot):
        return [up_piece(slot, s, k) for s in range(n_sub) for k in range(2)]

    def activate(slot):
        return [act_piece(slot, c) for c in range(fc // LANES)]

    def down(slot):
        return [down_piece(slot, n) for n in range(d // nsplit)]

    @pl.when(j == 0)
    def _():
        g = g_ref[...]
        zero = jnp.zeros((halo, d), BF16)
        ext_scr[0:halo, :] = jnp.where(tt > 0, _rms_scale(xp_ref[...], g).astype(BF16), zero)
        for r0 in range(0, tm, ACT_ROWS):
            rows = x1_ref[r0:r0 + ACT_ROWS, :]
            x2_ref[r0:r0 + ACT_ROWS, :] = rows
            ext_scr[halo + r0:halo + r0 + ACT_ROWS, :] = _rms_scale(rows, g).astype(BF16)
        ext_scr[halo + tm:, :] = jnp.where(tt < n_tt - 1, _rms_scale(xn_ref[...], g).astype(BF16), zero)
        _run_interleaved(up(0), [])

    @pl.when(j == 1)
    def _():
        _run_interleaved(up(1), activate(0))

    for par in range(2):
        @pl.when((j >= 2) & (j < n_fc) & (lax.rem(j, 2) == par))
        def _(par=par):
            _run_interleaved(up(par), activate(1 - par))
            _run_interleaved(down(par), [])

    @pl.when(j == n_fc)
    def _():
        _run_interleaved(down(n_fc % 2), activate((n_fc - 1) % 2))

    @pl.when(j == n_fc + 1)
    def _():
        _run_interleaved(down((n_fc - 1) % 2), [])
        x2b_ref[...] = x2_ref[...].astype(BF16)


def _ffn(x1, g, w_up, conv_w, conv_b, w_down, batch, seq, tm=512, fc=512):
    m, d = x1.shape
    d_ff = w_down.shape[0]
    halo = FFN_HALO
    tm = _tile(seq, tm)
    fc = _tile(d_ff, fc)
    sub = _tile(fc, MXU_WIDTH)
    nsplit = _tile(d, 512)
    n_tt = seq // tm
    n_fc = d_ff // fc
    hb = tm // halo
    n_hb = seq // halo
    assert n_fc >= 2 and tm % ACT_ROWS == 0
    main = pl.BlockSpec((tm, d), lambda b, t, j: (b * n_tt + t, 0), pipeline_mode=pl.Buffered(1))
    prev = pl.BlockSpec((halo, d), lambda b, t, j: (b * n_hb + jnp.maximum(t * hb - 1, 0), 0))
    nxt = pl.BlockSpec((halo, d), lambda b, t, j: (b * n_hb + jnp.minimum((t + 1) * hb, n_hb - 1), 0))
    chunk = lambda j, lag: jnp.clip(j - lag, 0, n_fc - 1)
    wv = pl.BlockSpec((d, fc), lambda b, t, j: (0, chunk(j, 0)))
    wg = pl.BlockSpec((d, fc), lambda b, t, j: (0, n_fc + chunk(j, 0)))
    cv = lambda rows: pl.BlockSpec((rows, fc), lambda b, t, j: (0, chunk(j, 1)))
    cg = lambda rows: pl.BlockSpec((rows, fc), lambda b, t, j: (0, n_fc + chunk(j, 1)))
    wd = pl.BlockSpec((fc, d), lambda b, t, j: (chunk(j, 2), 0))
    cb2 = conv_b.reshape(1, -1).astype(F32)
    cw = conv_w.astype(F32)
    kern = functools.partial(_ffn_kernel, n_tt=n_tt, n_fc=n_fc, tm=tm, sub=sub, nsplit=nsplit)
    return pl.pallas_call(
        kern,
        grid=(batch, n_tt, n_fc + 2),
        in_specs=[main, prev, nxt, pl.BlockSpec((1, d), lambda b, t, j: (0, 0)),
                  wv, wg, cv(3), cg(3), cv(1), cg(1), wd],
        out_specs=[main, main],
        out_shape=[jax.ShapeDtypeStruct((m, d), F32), jax.ShapeDtypeStruct((m, d), BF16)],
        scratch_shapes=[pltpu.VMEM((tm + 2 * halo, d), BF16),
                        pltpu.VMEM((2, 2, fc // LANES, tm + 2 * halo, LANES), F32),
                        pltpu.VMEM((2, tm, fc), BF16)],
        compiler_params=_cparams("parallel", "arbitrary", "arbitrary"),
        name="norm_conv_mlp",
    )(x1, x1, x1, g.reshape(1, d).astype(F32), w_up, w_up, cw, cw, cb2, cb2, w_down)


def _ple_kernel(xb_ref, x2_ref, p_ref, wp_ref, gp_ref, wg_ref, gf_ref, o_ref, e_scr, *, n_j, tn):
    j = pl.program_id(1)
    d = o_ref.shape[1]

    @pl.when(j == 0)
    def _():
        pb = p_ref[...].astype(BF16)
        ssq = jnp.zeros((pb.shape[0], 1), F32)
        for c in range(n_j):
            pe = jnp.dot(pb, wp_ref[:, c * tn:(c + 1) * tn], preferred_element_type=F32)
            e_scr[:, c * tn:(c + 1) * tn] = pe
            ssq = ssq + jnp.sum(pe * pe, axis=-1, keepdims=True)
        e_scr[...] = e_scr[...] * lax.rsqrt(ssq * (1.0 / d) + EPS) * gp_ref[...]

    gate = _sigmoid(jnp.dot(xb_ref[...], wg_ref[...], preferred_element_type=F32))
    for jj in range(n_j):
        @pl.when(j == jj)
        def _(jj=jj):
            o_ref[:, jj * tn:(jj + 1) * tn] = x2_ref[...] + gate * e_scr[:, jj * tn:(jj + 1) * tn]

    @pl.when(j == n_j - 1)
    def _():
        o_ref[...] = _rms_scale(o_ref[...], gf_ref[...])


def _ple_final(x2b, x2, p, w_ple, g_ple, w_gate, g_final, tm=512, tn=512):
    m, d = x2.shape
    kp = p.shape[1]
    tm = _tile(m, tm)
    tn = _tile(d, tn)
    n_j = d // tn
    row = lambda w: pl.BlockSpec((tm, w), lambda i, j: (i, 0))
    full = lambda r: pl.BlockSpec((r, d), lambda i, j: (0, 0))
    return pl.pallas_call(
        functools.partial(_ple_kernel, n_j=n_j, tn=tn),
        grid=(m // tm, n_j),
        in_specs=[row(d), pl.BlockSpec((tm, tn), lambda i, j: (i, j)), row(kp), full(kp), full(1),
                  pl.BlockSpec((d, tn), lambda i, j: (0, j)), full(1)],
        out_specs=row(d),
        out_shape=jax.ShapeDtypeStruct((m, d), F32),
        scratch_shapes=[pltpu.VMEM((tm, d), F32)],
        compiler_params=_cparams("parallel", "arbitrary"),
        name="ple_final",
    )(x2b, x2, p, w_ple, g_ple.reshape(1, d).astype(F32), w_gate, g_final.reshape(1, d).astype(F32))


def _encoder(x3, p4, wts):
    batch, seq, d = x3.shape
    m = batch * seq
    x = x3.reshape(m, d)
    n_q = wts["attn_sink"].shape[-1]
    attn_w = wts["w_o_attn"].shape[-2]
    d_rnn = wts["w_o_rnn"].shape[-2]
    hd = attn_w // n_q
    kv_w = (wts["w_in"].shape[-1] - attn_w - 2 * d_rnn) // 2
    n_kv = kv_w // hd
    qkv_w = attn_w + 2 * kv_w
    depth = wts["w_in"].shape[0]
    assert depth == 1, "the final norm is fused into the (single) layer's embedding kernel"
    ident = lambda acc: acc

    for l in range(depth):
        w_in = wts["w_in"][l]
        h = _rmsnorm(x, wts["g_mix"][l], BF16)
        qkv = _mm(h, w_in[:, :qkv_w].astype(BF16), ident, BF16, "in_qkv")
        xr = _mm(h, w_in[:, qkv_w:qkv_w + d_rnn].astype(BF16), ident, F32, "in_xr")
        gy = _mm(h, w_in[:, qkv_w + d_rnn:].astype(BF16), jax.nn.gelu, F32, "in_gelu_y")
        gates = _mm(h, wts["w_merge_gate"][l].astype(BF16), _sigmoid, F32, "merge_gates")

        attn = _attention(qkv, wts["attn_sink"][l], batch, seq, n_q, n_kv, hd)

        lru = (wts["rnn_conv_w"][l], wts["rnn_conv_b"][l], wts["lru_wa"][l], wts["lru_wx"][l],
               wts["lru_ba"][l], wts["lru_bx"][l], wts["lru_lambda"][l], batch, seq)
        h_fwd, xc = _rnn_direction(0, xr, *lru)
        rec = _rnn_direction(1, xc, *lru, hf=h_fwd, gy=gy)

        merged = _mm_merge(attn, rec, wts["w_o_attn"][l].astype(BF16), wts["w_o_rnn"][l].astype(BF16), gates, d)
        x1 = _mm(merged, wts["w_out"][l].astype(BF16), lambda acc, res: res + acc, F32, "out_residual", extra=x)
        x2, x2b = _ffn(x1, wts["g_ffn"][l], wts["w_up"][l].astype(BF16), wts["ffn_conv_w"][l],
                       wts["ffn_conv_b"][l], wts["w_down"][l].astype(BF16), batch, seq)
        y = _ple_final(x2b, x2, p4[l].reshape(m, -1), wts["w_ple"][l].astype(BF16), wts["g_ple"][l],
                       wts["w_ple_gate"][l].astype(BF16), wts["g_final"])
    return y.reshape(batch, seq, d)


def kernel(x_prompt, x_sample, p_prompt, p_sample, g_mix, w_in, rnn_conv_w, rnn_conv_b, lru_wa, lru_ba, lru_wx, lru_bx, lru_lambda, attn_sink, w_o_attn, w_o_rnn, w_merge_gate, w_out, g_ffn, w_up, ffn_conv_w, ffn_conv_b, w_down, w_ple_gate, w_ple, g_ple, g_final):
    wts = dict(g_mix=g_mix, w_in=w_in, rnn_conv_w=rnn_conv_w, rnn_conv_b=rnn_conv_b, lru_wa=lru_wa, lru_ba=lru_ba,
               lru_wx=lru_wx, lru_bx=lru_bx, lru_lambda=lru_lambda, attn_sink=attn_sink, w_o_attn=w_o_attn,
               w_o_rnn=w_o_rnn, w_merge_gate=w_merge_gate, w_out=w_out, g_ffn=g_ffn, w_up=w_up,
               ffn_conv_w=ffn_conv_w, ffn_conv_b=ffn_conv_b, w_down=w_down, w_ple_gate=w_ple_gate, w_ple=w_ple,
               g_ple=g_ple, g_final=g_final)
    return (_encoder(x_prompt, p_prompt, wts), _encoder(x_sample, p_sample, wts))
```

```python
import functools

import numpy as np
import jax
import jax.numpy as jnp
from jax import lax
from jax.experimental import pallas as pl
from jax.experimental.pallas import tpu as pltpu

F32 = jnp.float32
BF16 = jnp.bfloat16

EPS = 1e-6
NEG = -1e30
ATTN_BLOCK = 128
LRU_C = 8.0
RNN_HALO = 8
FFN_HALO = 16
LANES = 128
ACT_ROWS = 64
MXU_WIDTH = 256
V7X_VMEM_LIMIT_BYTES = 60 * 1024 * 1024


def _cparams(*sem):
    return pltpu.CompilerParams(dimension_semantics=sem, vmem_limit_bytes=V7X_VMEM_LIMIT_BYTES)


def _tile(dim, pref):
    t = min(dim, pref)
    while dim % t:
        t //= 2
    return t


def _run_interleaved(mxu_pieces, vpu_pieces):
    n_m, n_v = len(mxu_pieces), len(vpu_pieces)
    done_v = 0
    for i, piece in enumerate(mxu_pieces):
        piece()
        upto = (i + 1) * n_v // n_m
        for v in vpu_pieces[done_v:upto]:
            v()
        done_v = max(done_v, upto)
    for v in vpu_pieces[done_v:]:
        v()


def _sigmoid(x):
    return 0.5 * jnp.tanh(0.5 * x) + 0.5


def _rms_scale(x, g):
    ms = jnp.mean(x * x, axis=-1, keepdims=True)
    return x * lax.rsqrt(ms + EPS) * g


def _rmsnorm_kernel(x_ref, g_ref, o_ref):
    o_ref[...] = _rms_scale(x_ref[...].astype(F32), g_ref[...]).astype(o_ref.dtype)


def _rmsnorm(x, g, out_dtype, tm=256):
    m, d = x.shape
    tm = _tile(m, tm)
    return pl.pallas_call(
        _rmsnorm_kernel,
        grid=(m // tm,),
        in_specs=[pl.BlockSpec((tm, d), lambda i: (i, 0)), pl.BlockSpec((1, d), lambda i: (0, 0))],
        out_specs=pl.BlockSpec((tm, d), lambda i: (i, 0)),
        out_shape=jax.ShapeDtypeStruct((m, d), out_dtype),
        compiler_params=_cparams("parallel"),
        name="rmsnorm",
    )(x, g.reshape(1, d).astype(F32))


def _mm_kernel(*refs, epilogue):
    a_ref, b_ref, o_ref = refs[0], refs[1], refs[-1]
    acc = jnp.dot(a_ref[...], b_ref[...], preferred_element_type=F32)
    o_ref[...] = epilogue(acc, *[r[...] for r in refs[2:-1]]).astype(o_ref.dtype)


def _mm(a, b, epilogue, out_dtype, name, extra=None, tm=1024, tn=1024):
    m, k = a.shape
    n = b.shape[1]
    tm = _tile(m, tm)
    tn = _tile(n, tn)
    tile = pl.BlockSpec((tm, tn), lambda i, j: (i, j))
    extras = [] if extra is None else [extra]
    return pl.pallas_call(
        functools.partial(_mm_kernel, epilogue=epilogue),
        grid=(m // tm, n // tn),
        in_specs=[pl.BlockSpec((tm, k), lambda i, j: (i, 0)), pl.BlockSpec((k, tn), lambda i, j: (0, j))]
        + [tile] * len(extras),
        out_specs=tile,
        out_shape=jax.ShapeDtypeStruct((m, n), out_dtype),
        compiler_params=_cparams("parallel", "arbitrary"),
        name=name,
    )(a, b, *extras)


def _merge_kernel(a_ref, r_ref, wa_ref, wr_ref, ga_ref, gr_ref, o_ref):
    pa = jnp.dot(a_ref[...], wa_ref[...], preferred_element_type=F32)
    pr = jnp.dot(r_ref[...], wr_ref[...], preferred_element_type=F32)
    o_ref[...] = (ga_ref[...] * pa + gr_ref[...] * pr).astype(o_ref.dtype)


def _mm_merge(attn, rec, w_oa, w_or, gates, d, tm=1024, tn=512):
    m = attn.shape[0]
    tm = _tile(m, tm)
    tn = _tile(d, tn)
    n_j = d // tn
    return pl.pallas_call(
        _merge_kernel,
        grid=(m // tm, n_j),
        in_specs=[pl.BlockSpec((tm, attn.shape[1]), lambda i, j: (i, 0)),
                  pl.BlockSpec((tm, rec.shape[1]), lambda i, j: (i, 0)),
                  pl.BlockSpec((w_oa.shape[0], tn), lambda i, j: (0, j)),
                  pl.BlockSpec((w_or.shape[0], tn), lambda i, j: (0, j)),
                  pl.BlockSpec((tm, tn), lambda i, j: (i, j)),
                  pl.BlockSpec((tm, tn), lambda i, j: (i, n_j + j))],
        out_specs=pl.BlockSpec((tm, tn), lambda i, j: (i, j)),
        out_shape=jax.ShapeDtypeStruct((m, d), BF16),
        compiler_params=_cparams("parallel", "arbitrary"),
        name="merge",
    )(attn, rec, w_oa, w_or, gates, gates)


def _attn_kernel(sink_ref, q_ref, kp_ref, kc_ref, kn_ref, vp_ref, vc_ref, vn_ref, o_ref, bias_scr,
                 *, n_steps, n_kv, group, hd, slopes):
    blk = ATTN_BLOCK
    qr = q_ref.shape[0]
    nk = qr + 2 * blk
    n = pl.program_id(1)

    @pl.when((pl.program_id(0) == 0) & (n == 0))
    def _():
        t = lax.broadcasted_iota(jnp.int32, (qr, nk), 0)
        s = lax.broadcasted_iota(jnp.int32, (qr, nk), 1)
        dist = jnp.abs(s - blk - t)
        distf = dist.astype(F32)
        for hq in range(n_kv * group):
            bias_scr[hq * qr:(hq + 1) * qr, :] = jnp.where(dist <= blk, -slopes[hq] * distf, NEG)

    col = lax.broadcasted_iota(jnp.int32, (1, nk), 1)
    edge = jnp.where(((col < blk) & (n == 0)) | ((col >= blk + qr) & (n == n_steps - 1)), NEG, 0.0).astype(F32)
    scale = hd ** -0.5
    n_q = n_kv * group

    q_rows = [q_ref[:, hq * hd:(hq + 1) * hd] for hq in range(n_q)]
    scores = []
    for h in range(n_kv):
        q_h = jnp.concatenate(q_rows[h * group:(h + 1) * group], axis=0)
        k_h = jnp.concatenate([r[:, h * hd:(h + 1) * hd] for r in (kp_ref, kc_ref, kn_ref)], axis=0)
        scores.append(lax.dot_general(q_h, k_h, (((1,), (1,)), ((), ())), preferred_element_type=F32))
    s = jnp.concatenate(scores, axis=0) * scale + bias_scr[...] + edge
    sk = jnp.concatenate([jnp.full((qr, 1), sink_ref[hq], F32) for hq in range(n_q)], axis=0)
    m = jnp.maximum(jnp.max(s, axis=-1, keepdims=True), sk)
    e = jnp.exp(s - m)
    inv = 1.0 / (jnp.sum(e, axis=-1, keepdims=True) + jnp.exp(sk - m))
    eb = e.astype(BF16)
    rows = group * qr
    for h in range(n_kv):
        v_h = jnp.concatenate([r[:, h * hd:(h + 1) * hd] for r in (vp_ref, vc_ref, vn_ref)], axis=0)
        o = jnp.dot(eb[h * rows:(h + 1) * rows], v_h, preferred_element_type=F32) * inv[h * rows:(h + 1) * rows]
        for g in range(group):
            o_ref[:, (h * group + g) * hd:(h * group + g + 1) * hd] = o[g * qr:(g + 1) * qr].astype(o_ref.dtype)


def _attention(qkv, sink, batch, seq, n_q, n_kv, hd, q_blocks=1):
    blk = ATTN_BLOCK
    qr = q_blocks * blk
    nb = seq // blk
    n_steps = seq // qr
    attn_w, kv_w = n_q * hd, n_kv * hd
    assert attn_w % kv_w == 0 and seq % qr == 0
    kcol = attn_w // kv_w
    slopes = tuple(float(v) for v in (2.0 ** (-8.0 * (np.arange(n_q) + 1) / n_q)).astype(np.float32))

    def cur(w, colblk):
        return pl.BlockSpec((qr, w), lambda b, n: (b * n_steps + n, colblk))

    def side(colblk, first_blk):
        return pl.BlockSpec((blk, kv_w),
                            lambda b, n: (b * nb + jnp.clip(n * q_blocks + first_blk, 0, nb - 1), colblk))

    kern = functools.partial(_attn_kernel, n_steps=n_steps, n_kv=n_kv, group=n_q // n_kv, hd=hd, slopes=slopes)
    return pl.pallas_call(
        kern,
        grid=(batch, n_steps),
        in_specs=[pl.BlockSpec(memory_space=pltpu.SMEM),
                  cur(attn_w, 0),
                  side(kcol, -1), cur(kv_w, kcol), side(kcol, q_blocks),
                  side(kcol + 1, -1), cur(kv_w, kcol + 1), side(kcol + 1, q_blocks)],
        out_specs=cur(attn_w, 0),
        out_shape=jax.ShapeDtypeStruct((batch * seq, attn_w), BF16),
        scratch_shapes=[pltpu.VMEM((n_q * qr, qr + 2 * blk), F32)],
        compiler_params=_cparams("arbitrary", "arbitrary"),
        name="banded_gqa",
    )(sink.astype(F32), qkv, qkv, qkv, qkv, qkv, qkv, qkv)


def _lru_scan(a_scr, b_scr, carry_scr, n_slab, reverse):
    row = lax.broadcasted_iota(jnp.int32, a_scr.shape[1:], 0)

    def slab(i, carry):
        idx = (n_slab - 1 - i) if reverse else i
        a = a_scr[idx]
        b = b_scr[idx]
        for k in (1, 2, 4):
            keep = (row < 8 - k) if reverse else (row >= k)
            shift = (8 - k) if reverse else k
            a_sh = jnp.where(keep, pltpu.roll(a, shift, 0), 1.0)
            b_sh = jnp.where(keep, pltpu.roll(b, shift, 0), 0.0)
            b = b + a * b_sh
            a = a * a_sh
        h = a * carry + b
        b_scr[idx] = h
        return jnp.broadcast_to(h[0:1] if reverse else h[7:8], h.shape)

    carry_scr[...] = lax.fori_loop(0, n_slab, slab, carry_scr[...], unroll=n_slab)


def _rnn_kernel(*refs, reverse, n_tt, tt_rows, n_sub, sub_w):
    if reverse:
        (xc_ref, wa_ref, wx_ref, ba_ref, bx_ref, lam_ref, hf_ref, gy_ref, o_ref, a_scr, b_scr, carry_scr) = refs
    else:
        (xm_ref, xp_ref, xn_ref, cw_ref, cb_ref, wa_ref, wx_ref, ba_ref, bx_ref, lam_ref,
         o_ref, xc_ref, a_scr, b_scr, carry_scr) = refs
    halo = RNN_HALO
    tt = pl.program_id(2)

    @pl.when(tt == 0)
    def _():
        carry_scr[...] = jnp.zeros_like(carry_scr)

    if reverse:
        xc = xc_ref[...]
    else:
        xp = jnp.where(tt > 0, xp_ref[...], 0.0)
        xn = jnp.where(tt < n_tt - 1, xn_ref[...], 0.0)
        ext = jnp.concatenate([xp, xm_ref[...], xn], axis=0)
        n_ext = tt_rows + 2 * halo
        cw = cw_ref[...]
        xc = pltpu.roll(ext, 1, 0)[halo:halo + tt_rows] * cw[0:1]
        xc = xc + ext[halo:halo + tt_rows] * cw[1:2]
        xc = xc + pltpu.roll(ext, n_ext - 1, 0)[halo:halo + tt_rows] * cw[2:3]
        xc = xc + pltpu.roll(ext, n_ext - 2, 0)[halo:halo + tt_rows] * cw[3:4]
        xc = xc + cb_ref[...]
        xc_ref[...] = xc

    xcb = xc.astype(BF16)
    r_pre = jnp.concatenate(
        [jnp.dot(xcb[:, s * sub_w:(s + 1) * sub_w], wa_ref[s], preferred_element_type=F32) for s in range(n_sub)],
        axis=1)
    i_pre = jnp.concatenate(
        [jnp.dot(xcb[:, s * sub_w:(s + 1) * sub_w], wx_ref[s], preferred_element_type=F32) for s in range(n_sub)],
        axis=1)
    half_c_sp = (-0.5 * LRU_C) * jax.nn.softplus(-lam_ref[...])
    log_a = half_c_sp * jnp.tanh(r_pre + ba_ref[...]) + half_c_sp
    ig = 0.5 * jnp.tanh(i_pre + bx_ref[...]) + 0.5
    a = jnp.exp(log_a)
    one_minus_a2 = jnp.tanh(-log_a) * (1.0 + a * a)
    a_scr[...] = a.reshape(a_scr.shape)
    b_scr[...] = (jnp.sqrt(one_minus_a2) * (ig * xc)).reshape(b_scr.shape)

    _lru_scan(a_scr, b_scr, carry_scr, tt_rows // 8, reverse)
    hs = b_scr[...].reshape(tt_rows, -1)
    if reverse:
        o_ref[...] = ((hf_ref[...] + hs) * gy_ref[...]).astype(o_ref.dtype)
    else:
        o_ref[...] = hs


def _rnn_direction(d, x_in, conv_w, conv_b, wa, wx, ba, bx, lam, batch, seq, hf=None, gy=None, tt_rows=512, cb=512):
    m, d_rnn = x_in.shape
    sub_w = wa.shape[2]
    reverse = d == 1
    halo = RNN_HALO
    tt_rows = _tile(seq, tt_rows)
    cb = max(_tile(d_rnn, cb), sub_w)
    n_sub = cb // sub_w
    n_tt = seq // tt_rows
    hb = tt_rows // halo
    n_hb = seq // halo

    def te(t):
        return (n_tt - 1 - t) if reverse else t

    main = pl.BlockSpec((tt_rows, cb), lambda b, c, t: (b * n_tt + te(t), c))
    vec = lambda rows: pl.BlockSpec((rows, cb), lambda b, c, t: (0, c))
    wspec = pl.BlockSpec((n_sub, sub_w, sub_w), lambda b, c, t: (c, 0, 0))
    gate_specs = [wspec, wspec, vec(1), vec(1), vec(1)]
    gate_args = [(0.5 * wa[d]).astype(BF16), (0.5 * wx[d]).astype(BF16),
                 (0.5 * ba[d]).reshape(1, -1).astype(F32), (0.5 * bx[d]).reshape(1, -1).astype(F32),
                 lam[d].reshape(1, -1).astype(F32)]
    if reverse:
        in_specs = [main] + gate_specs + [main, main]
        args = [x_in] + gate_args + [hf, gy]
        out_specs = main
        out_shape = jax.ShapeDtypeStruct((m, d_rnn), BF16)
    else:
        prev = pl.BlockSpec((halo, cb), lambda b, c, t: (b * n_hb + jnp.maximum(t * hb - 1, 0), c))
        nxt = pl.BlockSpec((halo, cb), lambda b, c, t: (b * n_hb + jnp.minimum((t + 1) * hb, n_hb - 1), c))
        in_specs = [main, prev, nxt, vec(4), vec(1)] + gate_specs
        args = [x_in, x_in, x_in, conv_w.astype(F32), conv_b.reshape(1, -1).astype(F32)] + gate_args
        out_specs = [main, main]
        out_shape = [jax.ShapeDtypeStruct((m, d_rnn), F32), jax.ShapeDtypeStruct((m, d_rnn), F32)]
    kern = functools.partial(_rnn_kernel, reverse=reverse, n_tt=n_tt, tt_rows=tt_rows, n_sub=n_sub, sub_w=sub_w)
    return pl.pallas_call(
        kern,
        grid=(batch, d_rnn // cb, n_tt),
        in_specs=in_specs,
        out_specs=out_specs,
        out_shape=out_shape,
        scratch_shapes=[pltpu.VMEM((tt_rows // 8, 8, cb), F32), pltpu.VMEM((tt_rows // 8, 8, cb), F32),
                        pltpu.VMEM((8, cb), F32)],
        compiler_params=_cparams("arbitrary", "arbitrary", "arbitrary"),
        name="rglru_bwd" if reverse else "rglru_fwd",
    )(*args)


def _ffn_kernel(x1_ref, xp_ref, xn_ref, g_ref, wv_ref, wg_ref, cwv_ref, cwg_ref, cbv_ref, cbg_ref, wd_ref,
                x2_ref, x2b_ref, ext_scr, u_scr, act_scr, *, n_tt, n_fc, tm, sub, nsplit):
    halo = FFN_HALO
    tt = pl.program_id(1)
    j = pl.program_id(2)
    fc = act_scr.shape[2]
    n_sub = fc // sub
    n_lt = sub // LANES
    d = x2_ref.shape[1]

    def up_piece(slot, s, k):
        def run():
            w_ref = (wv_ref, wg_ref)[k]
            u = jnp.dot(ext_scr[...], w_ref[:, s * sub:(s + 1) * sub], preferred_element_type=F32)
            for c in range(n_lt):
                u_scr[slot, k, s * n_lt + c] = u[:, c * LANES:(c + 1) * LANES]
        return run

    def conv(slot, k, c, r0, cw_ref, cb_ref):
        cols = slice(c * LANES, (c + 1) * LANES)
        out = u_scr[slot, k, c, pl.ds(halo - 1 + r0, ACT_ROWS), :] * cw_ref[0:1, cols]
        out = out + u_scr[slot, k, c, pl.ds(halo + r0, ACT_ROWS), :] * cw_ref[1:2, cols]
        out = out + u_scr[slot, k, c, pl.ds(halo + 1 + r0, ACT_ROWS), :] * cw_ref[2:3, cols]
        return out + cb_ref[:, cols]

    def act_piece(slot, c):
        def run():
            for r0 in range(0, tm, ACT_ROWS):
                val = conv(slot, 0, c, r0, cwv_ref, cbv_ref)
                gate = conv(slot, 1, c, r0, cwg_ref, cbg_ref)
                act_scr[slot, r0:r0 + ACT_ROWS, c * LANES:(c + 1) * LANES] = (jax.nn.gelu(gate) * val).astype(BF16)
        return run

    def down_piece(slot, n):
        def run():
            ncols = slice(n * nsplit, (n + 1) * nsplit)
            x2_ref[:, ncols] += jnp.dot(act_scr[slot], wd_ref[:, ncols], preferred_element_type=F32)
        return run

    def up(slot):
        return [up_piece(slot, s, k) for s in range(n_sub) for k in range(2)]

    def activate(slot):
        return [act_piece(slot, c) for c in range(fc // LANES)]

    def down(slot):
        return [down_piece(slot, n) for n in range(d // nsplit)]

    @pl.when(j == 0)
    def _():
        g = g_ref[...]
        zero = jnp.zeros((halo, d), BF16)
        ext_scr[0:halo, :] = jnp.where(tt > 0, _rms_scale(xp_ref[...], g).astype(BF16), zero)
        for r0 in range(0, tm, ACT_ROWS):
            rows = x1_ref[r0:r0 + ACT_ROWS, :]
            x2_ref[r0:r0 + ACT_ROWS, :] = rows
            ext_scr[halo + r0:halo + r0 + ACT_ROWS, :] = _rms_scale(rows, g).astype(BF16)
        ext_scr[halo + tm:, :] = jnp.where(tt < n_tt - 1, _rms_scale(xn_ref[...], g).astype(BF16), zero)
        _run_interleaved(up(0), [])

    @pl.when(j == 1)
    def _():
        _run_interleaved(up(1), activate(0))

    for par in range(2):
        @pl.when((j >= 2) & (j < n_fc) & (lax.rem(j, 2) == par))
        def _(par=par):
            _run_interleaved(up(par), activate(1 - par))
            _run_interleaved(down(par), [])

    @pl.when(j == n_fc)
    def _():
        _run_interleaved(down(n_fc % 2), activate((n_fc - 1) % 2))

    @pl.when(j == n_fc + 1)
    def _():
        _run_interleaved(down((n_fc - 1) % 2), [])
        x2b_ref[...] = x2_ref[...].astype(BF16)


def _ffn(x1, g, w_up, conv_w, conv_b, w_down, batch, seq, tm=512, fc=512):
    m, d = x1.shape
    d_ff = w_down.shape[0]
    halo = FFN_HALO
    tm = _tile(seq, tm)
    fc = _tile(d_ff, fc)
    sub = _tile(fc, MXU_WIDTH)
    nsplit = _tile(d, 512)
    n_tt = seq // tm
    n_fc = d_ff // fc
    hb = tm // halo
    n_hb = seq // halo
    assert n_fc >= 2 and tm % ACT_ROWS == 0
    main = pl.BlockSpec((tm, d), lambda b, t, j: (b * n_tt + t, 0), pipeline_mode=pl.Buffered(1))
    prev = pl.BlockSpec((halo, d), lambda b, t, j: (b * n_hb + jnp.maximum(t * hb - 1, 0), 0))
    nxt = pl.BlockSpec((halo, d), lambda b, t, j: (b * n_hb + jnp.minimum((t + 1) * hb, n_hb - 1), 0))
    chunk = lambda j, lag: jnp.clip(j - lag, 0, n_fc - 1)
    wv = pl.BlockSpec((d, fc), lambda b, t, j: (0, chunk(j, 0)))
    wg = pl.BlockSpec((d, fc), lambda b, t, j: (0, n_fc + chunk(j, 0)))
    cv = lambda rows: pl.BlockSpec((rows, fc), lambda b, t, j: (0, chunk(j, 1)))
    cg = lambda rows: pl.BlockSpec((rows, fc), lambda b, t, j: (0, n_fc + chunk(j, 1)))
    wd = pl.BlockSpec((fc, d), lambda b, t, j: (chunk(j, 2), 0))
    cb2 = conv_b.reshape(1, -1).astype(F32)
    cw = conv_w.astype(F32)
    kern = functools.partial(_ffn_kernel, n_tt=n_tt, n_fc=n_fc, tm=tm, sub=sub, nsplit=nsplit)
    return pl.pallas_call(
        kern,
        grid=(batch, n_tt, n_fc + 2),
        in_specs=[main, prev, nxt, pl.BlockSpec((1, d), lambda b, t, j: (0, 0)),
                  wv, wg, cv(3), cg(3), cv(1), cg(1), wd],
        out_specs=[main, main],
        out_shape=[jax.ShapeDtypeStruct((m, d), F32), jax.ShapeDtypeStruct((m, d), BF16)],
        scratch_shapes=[pltpu.VMEM((tm + 2 * halo, d), BF16),
                        pltpu.VMEM((2, 2, fc // LANES, tm + 2 * halo, LANES), F32),
                        pltpu.VMEM((2, tm, fc), BF16)],
        compiler_params=_cparams("parallel", "arbitrary", "arbitrary"),
        name="norm_conv_mlp",
    )(x1, x1, x1, g.reshape(1, d).astype(F32), w_up, w_up, cw, cw, cb2, cb2, w_down)


def _ple_kernel(xb_ref, x2_ref, p_ref, wp_ref, gp_ref, wg_ref, gf_ref, o_ref, inv_scr, *, n_j, tn):
    j = pl.program_id(1)
    d = o_ref.shape[1]

    def pe_block(c):
        return jnp.dot(p_ref[...].astype(BF16), wp_ref[:, c * tn:(c + 1) * tn], preferred_element_type=F32)

    @pl.when(j == 0)
    def _():
        ssq = jnp.zeros((p_ref.shape[0], 1), F32)
        for c in range(n_j):
            pe = pe_block(c)
            ssq = ssq + jnp.sum(pe * pe, axis=-1, keepdims=True)
        inv_scr[...] = lax.rsqrt(ssq * (1.0 / d) + EPS)

    gate = _sigmoid(jnp.dot(xb_ref[...], wg_ref[...], preferred_element_type=F32))
    for jj in range(n_j):
        @pl.when(j == jj)
        def _(jj=jj):
            e = pe_block(jj) * inv_scr[...] * gp_ref[:, jj * tn:(jj + 1) * tn]
            o_ref[:, jj * tn:(jj + 1) * tn] = x2_ref[...] + gate * e

    @pl.when(j == n_j - 1)
    def _():
        o_ref[...] = _rms_scale(o_ref[...], gf_ref[...])


def _ple_final(x2b, x2, p, w_ple, g_ple, w_gate, g_final, tm=1024, tn=512):
    m, d = x2.shape
    kp = p.shape[1]
    tm = _tile(m, tm)
    tn = _tile(d, tn)
    n_j = d // tn
    row = lambda w: pl.BlockSpec((tm, w), lambda i, j: (i, 0))
    full = lambda r: pl.BlockSpec((r, d), lambda i, j: (0, 0))
    return pl.pallas_call(
        functools.partial(_ple_kernel, n_j=n_j, tn=tn),
        grid=(m // tm, n_j),
        in_specs=[row(d), pl.BlockSpec((tm, tn), lambda i, j: (i, j)), row(kp), full(kp), full(1),
                  pl.BlockSpec((d, tn), lambda i, j: (0, j)), full(1)],
        out_specs=pl.BlockSpec((tm, d), lambda i, j: (i, 0), pipeline_mode=pl.Buffered(1)),
        out_shape=jax.ShapeDtypeStruct((m, d), F32),
        scratch_shapes=[pltpu.VMEM((tm, 1), F32)],
        compiler_params=_cparams("parallel", "arbitrary"),
        name="ple_final",
    )(x2b, x2, p, w_ple, g_ple.reshape(1, d).astype(F32), w_gate, g_final.reshape(1, d).astype(F32))


def _encoder(x3, p4, wts):
    batch, seq, d = x3.shape
    m = batch * seq
    x = x3.reshape(m, d)
    n_q = wts["attn_sink"].shape[-1]
    attn_w = wts["w_o_attn"].shape[-2]
    d_rnn = wts["w_o_rnn"].shape[-2]
    hd = attn_w // n_q
    kv_w = (wts["w_in"].shape[-1] - attn_w - 2 * d_rnn) // 2
    n_kv = kv_w // hd
    qkv_w = attn_w + 2 * kv_w
    depth = wts["w_in"].shape[0]
    assert depth == 1, "the final norm is fused into the (single) layer's embedding kernel"
    ident = lambda acc: acc

    for l in range(depth):
        w_in = wts["w_in"][l]
        h = _rmsnorm(x, wts["g_mix"][l], BF16)
        qkv = _mm(h, w_in[:, :qkv_w].astype(BF16), ident, BF16, "in_qkv")
        xr = _mm(h, w_in[:, qkv_w:qkv_w + d_rnn].astype(BF16), ident, F32, "in_xr")
        gy = _mm(h, w_in[:, qkv_w + d_rnn:].astype(BF16), jax.nn.gelu, F32, "in_gelu_y")
        gates = _mm(h, wts["w_merge_gate"][l].astype(BF16), _sigmoid, F32, "merge_gates")

        attn = _attention(qkv, wts["attn_sink"][l], batch, seq, n_q, n_kv, hd)

        lru = (wts["rnn_conv_w"][l], wts["rnn_conv_b"][l], wts["lru_wa"][l], wts["lru_wx"][l],
               wts["lru_ba"][l], wts["lru_bx"][l], wts["lru_lambda"][l], batch, seq)
        h_fwd, xc = _rnn_direction(0, xr, *lru)
        rec = _rnn_direction(1, xc, *lru, hf=h_fwd, gy=gy)

        merged = _mm_merge(attn, rec, wts["w_o_attn"][l].astype(BF16), wts["w_o_rnn"][l].astype(BF16), gates, d)
        x1 = _mm(merged, wts["w_out"][l].astype(BF16), lambda acc, res: res + acc, F32, "out_residual", extra=x)
        x2, x2b = _ffn(x1, wts["g_ffn"][l], wts["w_up"][l].astype(BF16), wts["ffn_conv_w"][l],
                       wts["ffn_conv_b"][l], wts["w_down"][l].astype(BF16), batch, seq)
        y = _ple_final(x2b, x2, p4[l].reshape(m, -1), wts["w_ple"][l].astype(BF16), wts["g_ple"][l],
                       wts["w_ple_gate"][l].astype(BF16), wts["g_final"])
    return y.reshape(batch, seq, d)


def kernel(x_prompt, x_sample, p_prompt, p_sample, g_mix, w_in, rnn_conv_w, rnn_conv_b, lru_wa, lru_ba, lru_wx, lru_bx, lru_lambda, attn_sink, w_o_attn, w_o_rnn, w_merge_gate, w_out, g_ffn, w_up, ffn_conv_w, ffn_conv_b, w_down, w_ple_gate, w_ple, g_ple, g_final):
    wts = dict(g_mix=g_mix, w_in=w_in, rnn_conv_w=rnn_conv_w, rnn_conv_b=rnn_conv_b, lru_wa=lru_wa, lru_ba=lru_ba,
               lru_wx=lru_wx, lru_bx=lru_bx, lru_lambda=lru_lambda, attn_sink=attn_sink, w_o_attn=w_o_attn,
               w_o_rnn=w_o_rnn, w_merge_gate=w_merge_gate, w_out=w_out, g_ffn=g_ffn, w_up=w_up,
               ffn_conv_w=ffn_conv_w, ffn_conv_b=ffn_conv_b, w_down=w_down, w_ple_gate=w_ple_gate, w_ple=w_ple,
               g_ple=g_ple, g_final=g_final)
    return (_encoder(x_prompt, p_prompt, wts), _encoder(x_sample, p_sample, wts))
```
